```python
import math
import jax, jax.numpy as jnp
from jax import lax
import numpy as np

D_MODEL = 1024
BATCH = 1
SEQ = 16384
DEPTH = 2
DEC_BATCH = 128
DEC_SEQ = 4
PAST_LEN = 16384
PAGE_SIZE = 128

N_META = 16
BLOCK_Q = 128
LEAD_PAD = BLOCK_Q - N_META
ROPE_THETA = 10000.0
EPS = 1e-6
NEG_INF = -1e30

MIX_WIDTH = D_MODEL
DIFF_WIDTH = MIX_WIDTH // 2
MLA_WIDTH = MIX_WIDTH // 4
FOX_WIDTH = MIX_WIDTH - DIFF_WIDTH - MLA_WIDTH

DIFF_HEADS = 4
DIFF_V_DIM = DIFF_WIDTH // DIFF_HEADS
DIFF_QK_DIM = DIFF_V_DIM // 2
DIFF_SCALE = DIFF_QK_DIM ** -0.5

MLA_HEADS = 4
MLA_V_DIM = MLA_WIDTH // MLA_HEADS
MLA_NOPE_DIM = 64
MLA_ROPE_DIM = 32
MLA_Q_LORA = 192
MLA_KV_LORA = 128
MLA_SCALE = (MLA_NOPE_DIM + MLA_ROPE_DIM) ** -0.5

FOX_HEADS = 4
FOX_HEAD_DIM = FOX_WIDTH // FOX_HEADS
FOX_SCALE = FOX_HEAD_DIM ** -0.5
FORGET_BIAS_INIT = 3.0

SPLIT_SIZES = (
    DIFF_HEADS * 2 * DIFF_QK_DIM,
    2 * DIFF_QK_DIM,
    DIFF_V_DIM,
    MLA_Q_LORA,
    MLA_KV_LORA,
    MLA_ROPE_DIM,
    FOX_HEADS * FOX_HEAD_DIM,
    FOX_HEAD_DIM,
    FOX_HEAD_DIM,
    FOX_HEADS,
    MIX_WIDTH,
)
IN_WIDTH = sum(SPLIT_SIZES)
SPLIT_POINTS = tuple(int(v) for v in np.cumsum(SPLIT_SIZES)[:-1])
DIFF_ROW = 2 * DIFF_QK_DIM + DIFF_V_DIM
MLA_ROW = MLA_KV_LORA + MLA_ROPE_DIM
FOX_KV_ROW = 2 * FOX_HEAD_DIM

kernel_name = "hybrid_diff_mla_fox_parallel_heads_step"


def rms_norm(x, gain):
    xf = x.astype(jnp.float32)
    xf = xf * lax.rsqrt(jnp.mean(xf * xf, axis=-1, keepdims=True) + EPS)
    return xf.astype(x.dtype) * gain


def rotary(x, pos):
    d = x.shape[-1]
    inv_freq = ROPE_THETA ** (-jnp.arange(0, d, 2, dtype=jnp.float32) / d)
    ang = pos.astype(jnp.float32)[:, None] * inv_freq
    if x.ndim == 4:
        ang = ang[:, None, :]
    cos = jnp.cos(ang).astype(x.dtype)
    sin = jnp.sin(ang).astype(x.dtype)
    x1, x2 = jnp.split(x, 2, axis=-1)
    return jnp.concatenate([x1 * cos - x2 * sin, x2 * cos + x1 * sin], axis=-1)


def branch_inputs(x, pos, lp):
    b, t, _ = x.shape
    h = rms_norm(x, lp["norm_gain"])
    z = jnp.einsum("btd,de->bte", h, lp["w_in"])
    dq, dk, dv, cq, ckv, kr, fq, fk, fv, fz, gate = jnp.split(z, SPLIT_POINTS, axis=-1)
    dq = rotary(rms_norm(dq.reshape(b, t, 2 * DIFF_HEADS, DIFF_QK_DIM), lp["diff_q_gain"]), pos)
    dq = dq.reshape(b, t, DIFF_HEADS, 2, DIFF_QK_DIM)
    dk = rotary(rms_norm(dk.reshape(b, t, 2, DIFF_QK_DIM), lp["diff_k_gain"]), pos)
    diff_row = jnp.concatenate([dk.reshape(b, t, 2 * DIFF_QK_DIM), dv], axis=-1)
    mq = jnp.einsum("btr,rhe->bthe", rms_norm(cq, lp["mla_q_a_gain"]), lp["mla_w_uq"])
    mq_nope = rms_norm(mq[..., :MLA_NOPE_DIM], lp["mla_q_nope_gain"])
    mq_rope = rotary(rms_norm(mq[..., MLA_NOPE_DIM:], lp["mla_q_rope_gain"]), pos)
    mla_row = jnp.concatenate(
        [rms_norm(ckv, lp["mla_kv_gain"]), rotary(rms_norm(kr, lp["mla_k_rope_gain"]), pos)], axis=-1)
    fq = rms_norm(fq.reshape(b, t, FOX_HEADS, FOX_HEAD_DIM), lp["fox_q_gain"])
    fox_kv_row = jnp.concatenate([rms_norm(fk, lp["fox_k_gain"]), fv], axis=-1)
    logf = jax.nn.log_sigmoid((fz + lp["fox_f_bias"]).astype(jnp.float32))
    logf = jnp.where((pos >= 0)[None, :, None], logf, 0.0).astype(x.dtype)
    queries = (dq[:, :, :, 0], dq[:, :, :, 1], mq_nope, mq_rope, fq)
    return queries, (diff_row, mla_row, fox_kv_row, logf), gate


def expand_keys(rows, lp):
    diff_row, mla_row, fox_kv_row, logf = rows
    dk1 = diff_row[..., :DIFF_QK_DIM]
    dk2 = diff_row[..., DIFF_QK_DIM:2 * DIFF_QK_DIM]
    dv = diff_row[..., 2 * DIFF_QK_DIM:]
    ckv = mla_row[..., :MLA_KV_LORA]
    kr = mla_row[..., MLA_KV_LORA:]
    mk_nope = rms_norm(jnp.einsum("bsc,che->bshe", ckv, lp["mla_w_uk"]), lp["mla_k_nope_gain"])
    fk = fox_kv_row[..., :FOX_HEAD_DIM]
    fv = fox_kv_row[..., FOX_HEAD_DIM:]
    fcum = jnp.cumsum(logf.astype(jnp.float32), axis=1)
    return (dk1, dk2, dv, mk_nope, kr, ckv, fk, fv, fcum)


def attend(queries, qpos, keys, kpos, q_fcum, lp, lam, lam_init):
    dq1, dq2, mq_nope, mq_rope, fq = queries
    dk1, dk2, dv, mk_nope, kr, ckv, fk, fv, fcum = keys
    b, tq = dq1.shape[:2]
    mask = (kpos[None, :] <= qpos[:, None]) & (kpos >= 0)[None, :]

    def masked_softmax(scores):
        return jax.nn.softmax(jnp.where(mask, scores, NEG_INF), axis=-1)

    s1 = jnp.einsum("bqhd,bsd->bhqs", dq1, dk1).astype(jnp.float32) * DIFF_SCALE
    s2 = jnp.einsum("bqhd,bsd->bhqs", dq2, dk2).astype(jnp.float32) * DIFF_SCALE
    p_diff = masked_softmax(s1) - lam * masked_softmax(s2)
    do = jnp.einsum("bhqs,bse->bqhe", p_diff.astype(dv.dtype), dv)
    do = rms_norm(do, lp["diff_out_gain"]) * (1.0 - lam_init)
    sm = (jnp.einsum("bqhd,bshd->bhqs", mq_nope, mk_nope)
          + jnp.einsum("bqhr,bsr->bhqs", mq_rope, kr)).astype(jnp.float32) * MLA_SCALE
    ctx = jnp.einsum("bhqs,bsc->bqhc", masked_softmax(sm).astype(ckv.dtype), ckv)
    mo = jnp.einsum("bqhc,che->bqhe", ctx, lp["mla_w_uv"])
    decay = jnp.transpose(q_fcum, (0, 2, 1))[..., None] - jnp.transpose(fcum, (0, 2, 1))[:, :, None, :]
    sf = jnp.einsum("bqhd,bsd->bhqs", fq, fk).astype(jnp.float32) * FOX_SCALE + decay
    fo = jnp.einsum("bhqs,bse->bqhe", masked_softmax(sf).astype(fv.dtype), fv)
    return jnp.concatenate(
        [do.reshape(b, tq, DIFF_WIDTH), mo.reshape(b, tq, MLA_WIDTH), fo.reshape(b, tq, FOX_WIDTH)], axis=-1)


def prompt_layer(x, pos, lp, lam, lam_init):
    queries, rows, gate = branch_inputs(x, pos, lp)
    keys = expand_keys(rows, lp)
    fcum = keys[-1]
    n_blocks = x.shape[1] // BLOCK_Q

    def one_block(i):
        start = i * BLOCK_Q
        q_blk = tuple(lax.dynamic_slice_in_dim(q, start, BLOCK_Q, axis=1) for q in queries)
        qpos = lax.dynamic_slice_in_dim(pos, start, BLOCK_Q)
        q_cum = lax.dynamic_slice_in_dim(fcum, start, BLOCK_Q, axis=1)
        return attend(q_blk, qpos, keys, pos, q_cum, lp, lam, lam_init)

    mixed = lax.map(one_block, jnp.arange(n_blocks))
    mixed = jnp.moveaxis(mixed, 0, 1).reshape(x.shape[0], x.shape[1], MIX_WIDTH)
    y = x + jnp.einsum("btm,md->btd", mixed * jax.nn.silu(gate), lp["w_out"])
    return y, rows


def sample_layer(x, caches, layer, page_table, lp, lam, lam_init):
    past_len = page_table.shape[1] * PAGE_SIZE
    t_new = x.shape[1]
    pos = past_len + jnp.arange(t_new)
    queries, rows, gate = branch_inputs(x, pos, lp)
    kpos = jnp.arange(past_len + t_new)

    def one_sequence(args):
        q_seq, new_rows, pages = args
        full = tuple(
            jnp.concatenate([c[layer, pages].reshape(past_len, c.shape[-1]), r], axis=0)[None]
            for c, r in zip(caches, new_rows))
        keys = expand_keys(full, lp)
        q_cum = keys[-1][:, past_len:]
        return attend(tuple(q[None] for q in q_seq), pos, keys, kpos, q_cum, lp, lam, lam_init)[0]

    mixed = lax.map(one_sequence, (queries, rows, page_table))
    y = x + jnp.einsum("btm,md->btd", mixed * jax.nn.silu(gate), lp["w_out"])
    return y, rows


def setup_inputs(seed: int = 0) -> dict:
    key = jax.random.key(seed)
    keys = jax.random.split(key, 32)
    counter = [0]

    def nrm(shape, scale=1.0):
        k = keys[counter[0]]
        counter[0] += 1
        return scale * jax.random.normal(k, shape, jnp.float32)

    def gain(shape):
        return 1.0 + 0.05 * nrm(shape)

    n_pages = PAST_LEN // PAGE_SIZE
    n_used = DEC_BATCH * n_pages
    n_pool = n_used + n_used // 4
    perm = jax.random.permutation(keys[31], n_pool)
    page_table = perm[:n_used].reshape(DEC_BATCH, n_pages).astype(jnp.int32)
    return {
        "x_prompt": nrm((BATCH, SEQ, D_MODEL)),
        "x_sample": nrm((DEC_BATCH, DEC_SEQ, D_MODEL)),
        "cache_diff": nrm((DEPTH, n_pool, PAGE_SIZE, DIFF_ROW)),
        "cache_mla": nrm((DEPTH, n_pool, PAGE_SIZE, MLA_ROW)),
        "cache_fox_kv": nrm((DEPTH, n_pool, PAGE_SIZE, FOX_KV_ROW)),
        "cache_fox_logf": jax.nn.log_sigmoid(FORGET_BIAS_INIT + nrm((DEPTH, n_pool, PAGE_SIZE, FOX_HEADS))),
        "page_table": page_table,
        "meta_tokens": nrm((N_META, D_MODEL)),
        "norm_gain": gain((DEPTH, D_MODEL)),
        "w_in": nrm((DEPTH, D_MODEL, IN_WIDTH), D_MODEL ** -0.5),
        "w_out": nrm((DEPTH, MIX_WIDTH, D_MODEL), MIX_WIDTH ** -0.5),
        "diff_q_gain": gain((DEPTH, DIFF_QK_DIM)),
        "diff_k_gain": gain((DEPTH, DIFF_QK_DIM)),
        "diff_lambda": nrm((DEPTH, 4, DIFF_QK_DIM), 0.1),
        "diff_out_gain": gain((DEPTH, DIFF_V_DIM)),
        "mla_q_a_gain": gain((DEPTH, MLA_Q_LORA)),
        "mla_w_uq": nrm((DEPTH, MLA_Q_LORA, MLA_HEADS, MLA_NOPE_DIM + MLA_ROPE_DIM), MLA_Q_LORA ** -0.5),
        "mla_kv_gain": gain((DEPTH, MLA_KV_LORA)),
        "mla_w_uk": nrm((DEPTH, MLA_KV_LORA, MLA_HEADS, MLA_NOPE_DIM), MLA_KV_LORA ** -0.5),
        "mla_w_uv": nrm((DEPTH, MLA_KV_LORA, MLA_HEADS, MLA_V_DIM), MLA_KV_LORA ** -0.5),
        "mla_q_nope_gain": gain((DEPTH, MLA_NOPE_DIM)),
        "mla_q_rope_gain": gain((DEPTH, MLA_ROPE_DIM)),
        "mla_k_nope_gain": gain((DEPTH, MLA_NOPE_DIM)),
        "mla_k_rope_gain": gain((DEPTH, MLA_ROPE_DIM)),
        "fox_q_gain": gain((DEPTH, FOX_HEAD_DIM)),
        "fox_k_gain": gain((DEPTH, FOX_HEAD_DIM)),
        "fox_f_bias": FORGET_BIAS_INIT + nrm((DEPTH, FOX_HEADS), 0.1),
    }


def reference(x_prompt, x_sample, cache_diff, cache_mla, cache_fox_kv, cache_fox_logf, page_table,
              meta_tokens, norm_gain, w_in, w_out, diff_q_gain, diff_k_gain, diff_lambda, diff_out_gain,
              mla_q_a_gain, mla_w_uq, mla_kv_gain, mla_w_uk, mla_w_uv, mla_q_nope_gain, mla_q_rope_gain,
              mla_k_nope_gain, mla_k_rope_gain, fox_q_gain, fox_k_gain, fox_f_bias):
    b = x_prompt.shape[0]
    lead = jnp.zeros((b, LEAD_PAD, D_MODEL), x_prompt.dtype)
    meta = jnp.broadcast_to(meta_tokens[None].astype(x_prompt.dtype), (b, N_META, D_MODEL))
    xp = jnp.concatenate([lead, meta, x_prompt], axis=1)
    pos_p = jnp.arange(xp.shape[1]) - LEAD_PAD
    xs = x_sample
    caches = (cache_diff, cache_mla, cache_fox_kv, cache_fox_logf)
    rows_prompt = []
    rows_sample = []
    for l in range(DEPTH):
        lp = {
            "norm_gain": norm_gain[l], "w_in": w_in[l], "w_out": w_out[l],
            "diff_q_gain": diff_q_gain[l], "diff_k_gain": diff_k_gain[l], "diff_out_gain": diff_out_gain[l],
            "mla_q_a_gain": mla_q_a_gain[l], "mla_w_uq": mla_w_uq[l], "mla_kv_gain": mla_kv_gain[l],
            "mla_w_uk": mla_w_uk[l], "mla_w_uv": mla_w_uv[l], "mla_q_nope_gain": mla_q_nope_gain[l],
            "mla_q_rope_gain": mla_q_rope_gain[l], "mla_k_nope_gain": mla_k_nope_gain[l],
            "mla_k_rope_gain": mla_k_rope_gain[l], "fox_q_gain": fox_q_gain[l], "fox_k_gain": fox_k_gain[l],
            "fox_f_bias": fox_f_bias[l],
        }
        lam_init = 0.8 - 0.6 * math.exp(-0.3 * l)
        lq1, lk1, lq2, lk2 = diff_lambda[l, 0], diff_lambda[l, 1], diff_lambda[l, 2], diff_lambda[l, 3]
        lam = (jnp.exp(jnp.sum(lq1 * lk1, dtype=jnp.float32))
               - jnp.exp(jnp.sum(lq2 * lk2, dtype=jnp.float32)) + lam_init)
        xp, rp = prompt_layer(xp, pos_p, lp, lam, lam_init)
        xs, rs = sample_layer(xs, caches, l, page_table, lp, lam, lam_init)
        rows_prompt.append(tuple(r[:, LEAD_PAD:] for r in rp))
        rows_sample.append(rs)
    y_prompt = xp[:, LEAD_PAD + N_META:]
    y_sample = xs
    new_diff_prompt = jnp.stack([r[0] for r in rows_prompt])
    new_mla_prompt = jnp.stack([r[1] for r in rows_prompt])
    new_fox_kv_prompt = jnp.stack([r[2] for r in rows_prompt])
    new_fox_logf_prompt = jnp.stack([r[3] for r in rows_prompt])
    new_diff_sample = jnp.stack([r[0] for r in rows_sample])
    new_mla_sample = jnp.stack([r[1] for r in rows_sample])
    new_fox_kv_sample = jnp.stack([r[2] for r in rows_sample])
    new_fox_logf_sample = jnp.stack([r[3] for r in rows_sample])
    return (y_prompt, y_sample, new_diff_prompt, new_mla_prompt, new_fox_kv_prompt, new_fox_logf_prompt,
            new_diff_sample, new_mla_sample, new_fox_kv_sample, new_fox_logf_sample)
```

```python
import functools
import math

import numpy as np
import jax
import jax.numpy as jnp
from jax import lax
from jax.experimental import pallas as pl
from jax.experimental.pallas import tpu as pltpu

F32 = jnp.float32
BF16 = jnp.bfloat16

D_MODEL = 1024
N_META = 16
PAGE = 128
ROPE_THETA = 10000.0
EPS = 1e-6
NEG = -1e30
LOG2E = 1.4426950408889634

HEADS = 4
DQK = 64
DV = 128
Q_LORA = 192
KV_LORA = 128
NOPE = 64
ROPE = 32
FD = 64
MIX = 1024
DIFF_ROW = 256
MLA_ROW = 160
FOX_ROW = 128

SC_D = DQK ** -0.5 * LOG2E
SC_M = (NOPE + ROPE) ** -0.5 * LOG2E
SC_F = FD ** -0.5 * LOG2E

O_DQ, O_DQR, O_DK, O_DKR, O_DV, O_CQ, O_CKV, O_KR, O_KRR, O_FQ, O_FKV, O_FZ, O_GATE, NP1 = (
    0, 512, 1024, 1152, 1280, 1408, 1664, 1792, 1920, 2048, 2304, 2432, 2560, 3584)

VMEM_LIMIT = 56 * 1024 * 1024


def _rot_idx(d):
    e = np.arange(d)
    return (e + d // 2) % d, np.where(e < d // 2, -1.0, 1.0).astype(np.float32)


P64, S64 = _rot_idx(64)
P32, S32 = _rot_idx(32)


def _in_cols():
    neg = lambda n: -np.ones(n, np.int64)
    dq = [(2 * h + j) * 64 + e for j in range(2) for h in range(HEADS) for e in range(64)]
    dqr = [(2 * h + j) * 64 + P64[e] for j in range(2) for h in range(HEADS) for e in range(64)]
    dk = [512 + g * 64 + e for g in range(2) for e in range(64)]
    dkr = [512 + g * 64 + P64[e] for g in range(2) for e in range(64)]
    parts = [
        np.array(dq), np.array(dqr), np.array(dk), np.array(dkr), 640 + np.arange(128),
        np.concatenate([768 + np.arange(192), neg(64)]), 960 + np.arange(128),
        np.concatenate([1088 + np.arange(32), neg(96)]), np.concatenate([1088 + P32, neg(96)]),
        1120 + np.arange(256), 1376 + np.arange(128), np.concatenate([1504 + np.arange(4), neg(124)]),
        1508 + np.arange(1024),
    ]
    cols = np.concatenate(parts)
    assert cols.shape[0] == NP1
    return cols


IN_COLS = _in_cols()


def _block_diag_mean(n, g):
    m = np.zeros((n, n), np.float32)
    for i in range(n // g):
        m[i * g:(i + 1) * g, i * g:(i + 1) * g] = 1.0 / g
    return m


def _row(v, n=None):
    v = jnp.asarray(v, F32).reshape(1, -1)
    if n is not None and v.shape[1] < n:
        v = jnp.pad(v, ((0, 0), (0, n - v.shape[1])))
    return v


def _layer_params(l, norm_gain, w_in, w_out, diff_q_gain, diff_k_gain, diff_lambda, diff_out_gain,
                  mla_q_a_gain, mla_w_uq, mla_kv_gain, mla_w_uk, mla_w_uv, mla_q_nope_gain, mla_q_rope_gain,
                  mla_k_nope_gain, mla_k_rope_gain, fox_q_gain, fox_k_gain, fox_f_bias):
    cols = jnp.asarray(np.maximum(IN_COLS, 0), jnp.int32)
    wp = jnp.where(jnp.asarray(IN_COLS >= 0)[None, :], jnp.take(w_in[l], cols, axis=1), 0.0).astype(BF16)
    wuq = mla_w_uq[l]
    nope = wuq[:, :, :NOPE].reshape(Q_LORA, HEADS * NOPE)
    rope = wuq[:, :, NOPE:]
    rope_rot = rope[:, :, P32].reshape(Q_LORA, HEADS * ROPE)
    rope = rope.reshape(Q_LORA, HEADS * ROPE)
    wuq2 = jnp.pad(jnp.concatenate([nope, rope, rope_rot], axis=1), ((0, 256 - Q_LORA), (0, 0))).astype(BF16)
    wuk = jnp.pad(mla_w_uk[l], ((0, 0), (0, 0), (0, 64))).reshape(KV_LORA, HEADS * 128).astype(BF16)
    wuk_dense = mla_w_uk[l].reshape(KV_LORA, HEADS * NOPE).astype(BF16)
    wuv = jnp.zeros((HEADS, KV_LORA, HEADS * 64), F32)
    for h in range(HEADS):
        wuv = wuv.at[h, :, h * 64:(h + 1) * 64].set(mla_w_uv[l][:, h, :])
    wuv = wuv.astype(BF16)
    lam_init = 0.8 - 0.6 * math.exp(-0.3 * l)
    lq1, lk1, lq2, lk2 = diff_lambda[l, 0], diff_lambda[l, 1], diff_lambda[l, 2], diff_lambda[l, 3]
    lam = (jnp.exp(jnp.sum(lq1 * lk1, dtype=F32)) - jnp.exp(jnp.sum(lq2 * lk2, dtype=F32)) + lam_init)
    s64 = jnp.asarray(S64)
    s32 = jnp.asarray(S32)
    p64 = jnp.asarray(P64)
    p32 = jnp.asarray(P32)
    vec = dict(
        ng=_row(norm_gain[l]),
        dq_c=_row(jnp.tile(diff_q_gain[l], 8) * SC_D),
        dq_s=_row(jnp.tile(diff_q_gain[l][p64] * s64, 8) * SC_D),
        dk_c=_row(jnp.tile(diff_k_gain[l], 2)),
        dk_s=_row(jnp.tile(diff_k_gain[l][p64] * s64, 2)),
        cq_g=_row(mla_q_a_gain[l], 256),
        ckv_g=_row(mla_kv_gain[l]),
        kr_c=_row(mla_k_rope_gain[l], 128),
        kr_s=_row(mla_k_rope_gain[l][p32] * s32, 128),
        qn_g=_row(jnp.tile(mla_q_nope_gain[l], HEADS) * SC_M),
        qr_c=_row(jnp.tile(mla_q_rope_gain[l], HEADS) * SC_M),
        qr_s=_row(jnp.tile(mla_q_rope_gain[l][p32] * s32, HEADS) * SC_M),
        kn_g=_row(jnp.tile(jnp.pad(mla_k_nope_gain[l], (0, 64)), HEADS)),
        fq_g=_row(jnp.tile(fox_q_gain[l], HEADS) * SC_F),
        fk_g=_row(fox_k_gain[l], 128),
        fz_b=_row(fox_f_bias[l], 128),
    )
    return dict(wp=wp, wuq2=wuq2, wuk=wuk, wuk_dense=wuk_dense, wuv=wuv, wout=w_out[l].astype(BF16), vec=vec,
                lam=lam, lam_init=lam_init, out_gain=diff_out_gain[l], kn_gain=mla_k_nope_gain[l])


VEC_NAMES = ("ng", "dq_c", "dq_s", "dk_c", "dk_s", "cq_g", "ckv_g", "kr_c", "kr_s", "qn_g", "qr_c", "qr_s",
             "kn_g", "fq_g", "fk_g", "fz_b")


def _inv_freq_rows():
    f64 = ROPE_THETA ** (-jnp.arange(0, 64, 2, dtype=F32) / 64)
    f32 = ROPE_THETA ** (-jnp.arange(0, 32, 2, dtype=F32) / 32)
    return jnp.tile(f64, 4).reshape(1, 128), jnp.tile(f32, 8).reshape(1, 128)


def _dotf(a, b):
    return jnp.dot(a, b, preferred_element_type=F32)


def _split3(x):
    hi = x.astype(BF16)
    r1 = x - hi.astype(F32)
    mid = r1.astype(BF16)
    lo = (r1 - mid.astype(F32)).astype(BF16)
    return hi, mid, lo


def _proj_kernel(prompt, x_ref, pos_ref, inv64_ref, inv32_ref, wp_ref, wuq_ref, wuk_ref, g64_ref, g32_ref, tri_ref,
                 ng, dq_c, dq_s, dk_c, dk_s, cq_g, ckv_g, kr_c, kr_s, qn_g, qr_c, qr_s, kn_g, fq_g, fk_g, fz_b,
                 *refs):
    if prompt:
        (qd_ref, qm_ref, qf_ref, kd_ref, vdt_ref, km_ref, ckvt_ref, kf_ref, vft_ref,
         drow_ref, mrow_ref, frow_ref, logf_ref, sg_ref, carry_ref) = refs
    else:
        (qd_ref, qm_ref, qf_ref, drow_ref, mrow_ref, frow_ref, logf_ref, sg_ref) = refs
    tm = x_ref.shape[0]

    x = x_ref[...]
    h = (x * lax.rsqrt(jnp.mean(x * x, axis=-1, keepdims=True) + EPS)) * ng[...]
    z = _dotf(h.astype(BF16), wp_ref[...])

    pos = pos_ref[...]
    a64 = pos * inv64_ref[...]
    a32 = pos * inv32_ref[...]
    c64, s64 = jnp.cos(a64), jnp.sin(a64)
    c32, s32 = jnp.cos(a32), jnp.sin(a32)
    g64 = g64_ref[...]
    g64h = g64_ref[0:128, 0:128]
    g32 = g32_ref[...]

    def inv_rms_g(v, g):
        return lax.rsqrt(_dotf((v * v).astype(BF16), g) + EPS)

    dq_parts = []
    for c in range(2):
        zq = z[:, O_DQ + c * 256:O_DQ + (c + 1) * 256]
        zr = z[:, O_DQR + c * 256:O_DQR + (c + 1) * 256]
        r = inv_rms_g(zq, g64)
        cc = jnp.concatenate([c64, c64], axis=1)
        ss = jnp.concatenate([s64, s64], axis=1)
        gc = dq_c[:, c * 256:(c + 1) * 256]
        gs = dq_s[:, c * 256:(c + 1) * 256]
        dq_parts.append(r * (zq * (gc * cc) + zr * (gs * ss)))
    zk = z[:, O_DK:O_DK + 128]
    zkr = z[:, O_DKR:O_DKR + 128]
    dk = inv_rms_g(zk, g64h) * (zk * (dk_c[...] * c64) + zkr * (dk_s[...] * s64))
    dv = z[:, O_DV:O_DV + 128]
    drow_ref[:, 0:128] = dk
    drow_ref[:, 128:256] = dv

    zc = z[:, O_CQ:O_CQ + 256]
    rc = lax.rsqrt(jnp.sum(zc * zc, axis=-1, keepdims=True) * (1.0 / Q_LORA) + EPS)
    mq = _dotf(((zc * rc) * cq_g[...]).astype(BF16), wuq_ref[...])
    qn_raw = mq[:, 0:256]
    qn = (qn_raw * inv_rms_g(qn_raw, g64)) * qn_g[...]
    qr_raw = mq[:, 256:384]
    qr_rot = mq[:, 384:512]
    qr = inv_rms_g(qr_raw, g32) * (qr_raw * (qr_c[...] * c32) + qr_rot * (qr_s[...] * s32))
    zv = z[:, O_CKV:O_CKV + 128]
    ckvn = (zv * lax.rsqrt(jnp.mean(zv * zv, axis=-1, keepdims=True) + EPS)) * ckv_g[...]
    zr0 = z[:, O_KR:O_KR + 128]
    zr1 = z[:, O_KRR:O_KRR + 128]
    rk = lax.rsqrt(jnp.sum(zr0 * zr0, axis=-1, keepdims=True) * (1.0 / ROPE) + EPS)
    krn = rk * (zr0 * (kr_c[...] * c32) + zr1 * (kr_s[...] * s32))
    mrow_ref[:, 0:128] = ckvn
    mrow_ref[:, 128:160] = krn[:, 0:32]

    zf = z[:, O_FQ:O_FQ + 256]
    fq = (zf * inv_rms_g(zf, g64)) * fq_g[...]
    fkv = z[:, O_FKV:O_FKV + 128]
    lane = lax.broadcasted_iota(jnp.int32, (tm, 128), 1)
    fkn = (fkv * inv_rms_g(fkv, g64h)) * fk_g[...]
    frow = jnp.where(lane < 64, fkn, fkv)
    frow_ref[...] = frow
    zz = z[:, O_FZ:O_FZ + 128] + fz_b[...]
    logf = jnp.minimum(zz, 0.0) - jnp.log1p(jnp.exp(-jnp.abs(zz)))
    logf = jnp.where((lane < HEADS) & (pos >= 0.0), logf, 0.0)
    logf_ref[...] = logf

    gate = z[:, O_GATE:O_GATE + MIX]
    sg_ref[...] = (gate * (1.0 / (1.0 + jnp.exp(-gate)))).astype(BF16)

    if not prompt:
        qd_ref[:, 0:256] = dq_parts[0].astype(BF16)
        qd_ref[:, 256:512] = dq_parts[1].astype(BF16)
        qm_ref[:, 0:256] = qn.astype(BF16)
        qm_ref[:, 256:384] = qr.astype(BF16)
        qf_ref[...] = fq.astype(BF16)
        return

    z64 = jnp.zeros((64, tm), BF16)
    for j in range(2):
        qt = dq_parts[j].T
        for hh in range(HEADS):
            blk = qt[hh * 64:(hh + 1) * 64, :].astype(BF16)
            qd_ref[j * HEADS + hh, j * 64:(j + 1) * 64, :] = blk
            qd_ref[j * HEADS + hh, (1 - j) * 64:(2 - j) * 64, :] = z64
    kd_ref[...] = dk.astype(BF16)
    vdt_ref[...] = dv.T.astype(BF16)

    qnt = qn.T
    qrt = qr.T
    for hh in range(HEADS):
        qm_ref[hh, 0:64, :] = qnt[hh * 64:(hh + 1) * 64, :].astype(BF16)
        qm_ref[hh, 64:96, :] = qrt[hh * 32:(hh + 1) * 32, :].astype(BF16)
        qm_ref[hh, 96:128, :] = jnp.zeros((32, tm), BF16)
    kp = _dotf(ckvn.astype(BF16), wuk_ref[...])
    kr_sh = pltpu.roll(krn, 64, 1)
    for hh in range(HEADS):
        kph = kp[:, hh * 128:(hh + 1) * 128]
        knh = (kph * inv_rms_g(kph, g64h)) * kn_g[:, hh * 128:(hh + 1) * 128]
        km_ref[hh] = (knh + kr_sh).astype(BF16)
    ckvt_ref[...] = ckvn.T.astype(BF16)

    @pl.when(pl.program_id(0) == 0)
    def _():
        carry_ref[...] = jnp.zeros_like(carry_ref)
    hi, mid, lo = _split3(logf)
    tri = tri_ref[...]
    fcum = carry_ref[0:1, :] + (_dotf(tri, hi) + _dotf(tri, mid) + _dotf(tri, lo))
    carry_ref[...] = carry_ref[...] + jnp.sum(logf, axis=0, keepdims=True)
    f2 = fcum * (-LOG2E)
    fh, fm, fl = _split3(f2)
    aug = fh.astype(F32) + pltpu.roll(fm.astype(F32), 4, 1) + pltpu.roll(fl.astype(F32), 8, 1)
    kf_ref[...] = (jnp.where(lane < 64, fkn, 0.0) + pltpu.roll(aug, 64, 1)).astype(BF16)
    frt = frow.T
    vft_ref[...] = frt[64:128, :].astype(BF16)
    fqt = fq.T
    rowi = lax.broadcasted_iota(jnp.int32, (64, tm), 0)
    for hh in range(HEADS):
        qf_ref[hh, 0:64, :] = fqt[hh * 64:(hh + 1) * 64, :].astype(BF16)
        pat = (rowi == hh) | (rowi == 4 + hh) | (rowi == 8 + hh)
        qf_ref[hh, 64:128, :] = jnp.where(pat, 1.0, 0.0).astype(BF16)


def _const_spec(shape):
    nd = len(shape)
    return pl.BlockSpec(shape, lambda i, _n=nd: (0,) * _n)


def _proj(x, pos, lp, consts, *, prompt, tm):
    t = x.shape[0]
    assert t % tm == 0
    n = t // tm
    inv64, inv32, g64, g32, tri = consts
    vecs = [lp["vec"][k] for k in VEC_NAMES]
    wuk = lp["wuk"]
    in_specs = [
        pl.BlockSpec((tm, D_MODEL), lambda i: (i, 0)),
        pl.BlockSpec((tm, 1), lambda i: (i, 0)),
        _const_spec(inv64.shape), _const_spec(inv32.shape),
        _const_spec(lp["wp"].shape), _const_spec(lp["wuq2"].shape), _const_spec(wuk.shape),
        _const_spec(g64.shape), _const_spec(g32.shape), _const_spec(tri.shape),
    ] + [_const_spec(v.shape) for v in vecs]
    tok = lambda w: pl.BlockSpec((tm, w), lambda i: (i, 0))
    feat = lambda r: pl.BlockSpec((r, tm), lambda i: (0, i))
    feat3 = lambda a, r: pl.BlockSpec((a, r, tm), lambda i: (0, 0, i))
    rows_shapes = [jax.ShapeDtypeStruct((t, DIFF_ROW), F32), jax.ShapeDtypeStruct((t, MLA_ROW), F32),
                   jax.ShapeDtypeStruct((t, FOX_ROW), F32), jax.ShapeDtypeStruct((t, 128), F32),
                   jax.ShapeDtypeStruct((t, MIX), BF16)]
    rows_specs = [tok(DIFF_ROW), tok(MLA_ROW), tok(FOX_ROW), tok(128), tok(MIX)]
    if prompt:
        out_shape = [jax.ShapeDtypeStruct((8, 128, t), BF16), jax.ShapeDtypeStruct((4, 128, t), BF16),
                     jax.ShapeDtypeStruct((4, 128, t), BF16), jax.ShapeDtypeStruct((t, 128), BF16),
                     jax.ShapeDtypeStruct((128, t), BF16), jax.ShapeDtypeStruct((HEADS, t, 128), BF16),
                     jax.ShapeDtypeStruct((128, t), BF16), jax.ShapeDtypeStruct((t, 128), BF16),
                     jax.ShapeDtypeStruct((64, t), BF16)] + rows_shapes
        out_specs = [feat3(8, 128), feat3(4, 128), feat3(4, 128), tok(128), feat(128),
                     pl.BlockSpec((HEADS, tm, 128), lambda i: (0, i, 0)), feat(128),
                     tok(128), feat(64)] + rows_specs
        scratch = [pltpu.VMEM((8, 128), F32)]
    else:
        out_shape = [jax.ShapeDtypeStruct((t, 512), BF16), jax.ShapeDtypeStruct((t, 384), BF16),
                     jax.ShapeDtypeStruct((t, 256), BF16)] + rows_shapes
        out_specs = [tok(512), tok(384), tok(256)] + rows_specs
        scratch = []
    return pl.pallas_call(
        functools.partial(_proj_kernel, prompt),
        grid=(n,),
        in_specs=in_specs,
        out_specs=out_specs,
        out_shape=out_shape,
        scratch_shapes=scratch,
        compiler_params=pltpu.CompilerParams(dimension_semantics=("arbitrary",), vmem_limit_bytes=VMEM_LIMIT),
        name="proj_prompt" if prompt else "proj_sample",
    )(x, pos, inv64, inv32, lp["wp"], lp["wuq2"], wuk, g64, g32, tri, *vecs)


def _out_kernel(x_ref, mix_ref, sg_ref, w_ref, y_ref):
    g = (mix_ref[...] * sg_ref[...].astype(F32)).astype(BF16)
    y_ref[...] = x_ref[...] + _dotf(g, w_ref[...])


def _out_proj(x, mixed, sg, wout, *, tm):
    t = x.shape[0]
    assert t % tm == 0
    tok = lambda w: pl.BlockSpec((tm, w), lambda i: (i, 0))
    return pl.pallas_call(
        _out_kernel,
        grid=(t // tm,),
        in_specs=[tok(D_MODEL), tok(MIX), tok(MIX), _const_spec(wout.shape)],
        out_specs=tok(D_MODEL),
        out_shape=jax.ShapeDtypeStruct((t, D_MODEL), F32),
        compiler_params=pltpu.CompilerParams(dimension_semantics=("arbitrary",), vmem_limit_bytes=VMEM_LIMIT),
        name="out_proj",
    )(x, mixed, sg, wout)


def _attn_kernel(qi_tab, kj_tab, qd_ref, qm_ref, qf_ref, kd_ref, vdt_ref, km_ref, ckvt_ref, kf_ref, vft_ref,
                 wuv_ref, gcol_ref, scal_ref, mix_ref, m_ref, l_ref, accd_ref, accm_ref, accf_ref):
    tb = kd_ref.shape[0]
    i = pl.program_id(0)
    qi = qi_tab[i]
    kj = kj_tab[i]

    @pl.when(kj == 0)
    def _():
        m_ref[...] = jnp.full(m_ref.shape, NEG, F32)
        l_ref[...] = jnp.zeros_like(l_ref)
        accd_ref[...] = jnp.zeros_like(accd_ref)
        accm_ref[...] = jnp.zeros_like(accm_ref)
        accf_ref[...] = jnp.zeros_like(accf_ref)

    def step(masked):
        if masked:
            mask = (lax.broadcasted_iota(jnp.int32, (tb, tb), 0) <= lax.broadcasted_iota(jnp.int32, (tb, tb), 1))

        def upd(idx, s, vt, acc_ref, aidx):
            if masked:
                s = jnp.where(mask, s, NEG)
            m_prev = m_ref[idx]
            m_new = jnp.maximum(m_prev, jnp.max(s, axis=0, keepdims=True))
            alpha = jnp.exp2(m_prev - m_new)
            p = jnp.exp2(s - m_new)
            l_ref[idx] = alpha * l_ref[idx] + jnp.sum(p, axis=0, keepdims=True)
            m_ref[idx] = m_new
            acc_ref[aidx] = alpha * acc_ref[aidx] + _dotf(vt, p.astype(BF16))

        def body(h, c):
            kd = kd_ref[...]
            upd(h, _dotf(kd, qd_ref[h]), vdt_ref[...], accd_ref, h)
            upd(HEADS + h, _dotf(kd, qd_ref[HEADS + h]), vdt_ref[...], accd_ref, HEADS + h)
            upd(2 * HEADS + h, _dotf(km_ref[h], qm_ref[h]), ckvt_ref[...], accm_ref, h)
            upd(3 * HEADS + h, _dotf(kf_ref[...], qf_ref[h]), vft_ref[...], accf_ref, h)
            return c

        lax.fori_loop(0, HEADS, body, 0)

    @pl.when(kj < qi)
    def _():
        step(False)

    @pl.when(kj == qi)
    def _():
        step(True)
        lam = scal_ref[0]
        oml = scal_ref[1]
        gcol = gcol_ref[...]
        mo = jnp.zeros((tb, HEADS * 64), F32)
        fo = []
        for h in range(HEADS):
            o1 = accd_ref[h] * (1.0 / l_ref[h])
            o2 = accd_ref[HEADS + h] * (1.0 / l_ref[HEADS + h])
            do = o1 - lam * o2
            do = (do * lax.rsqrt(jnp.mean(do * do, axis=0, keepdims=True) + EPS)) * gcol * oml
            mix_ref[:, h * DV:(h + 1) * DV] = do.T
            ctx = (accm_ref[h] * (1.0 / l_ref[2 * HEADS + h])).T
            mo = mo + _dotf(ctx.astype(BF16), wuv_ref[h])
            fo.append(accf_ref[h] * (1.0 / l_ref[3 * HEADS + h]))
        mix_ref[:, 512:768] = mo
        mix_ref[:, 768:1024] = jnp.concatenate(fo, axis=0).T


def _pair_tables(nb):
    qi = np.concatenate([np.full(q + 1, q, np.int32) for q in range(nb)])
    kj = np.concatenate([np.arange(q + 1, dtype=np.int32) for q in range(nb)])
    return jnp.asarray(qi), jnp.asarray(kj)


def _prompt_attention(ops, lp, *, tb):
    qd, qm, qf, kd, vdt, km, ckvt, kf, vft = ops
    t = kd.shape[0]
    nb = t // tb
    qi_tab, kj_tab = _pair_tables(nb)
    scal = jnp.stack([lp["lam"], jnp.asarray(1.0 - lp["lam_init"], F32)]).astype(F32)
    gcol = lp["out_gain"].reshape(DV, 1).astype(F32)
    qspec = lambda a: pl.BlockSpec((a, 128, tb), lambda i, qt, kt: (0, 0, qt[i]))
    ktok = lambda w: pl.BlockSpec((tb, w), lambda i, qt, kt: (kt[i], 0))
    kfeat = lambda r: pl.BlockSpec((r, tb), lambda i, qt, kt: (0, kt[i]))
    grid_spec = pltpu.PrefetchScalarGridSpec(
        num_scalar_prefetch=2,
        grid=(int(qi_tab.shape[0]),),
        in_specs=[qspec(8), qspec(4), qspec(4), ktok(128), kfeat(128),
                  pl.BlockSpec((HEADS, tb, 128), lambda i, qt, kt: (0, kt[i], 0)), kfeat(128), ktok(128), kfeat(64),
                  pl.BlockSpec((HEADS, KV_LORA, 256), lambda i, qt, kt: (0, 0, 0)),
                  pl.BlockSpec((DV, 1), lambda i, qt, kt: (0, 0)),
                  pl.BlockSpec(memory_space=pltpu.SMEM)],
        out_specs=pl.BlockSpec((tb, MIX), lambda i, qt, kt: (qt[i], 0)),
        scratch_shapes=[pltpu.VMEM((16, 1, tb), F32), pltpu.VMEM((16, 1, tb), F32),
                        pltpu.VMEM((8, DV, tb), F32), pltpu.VMEM((4, KV_LORA, tb), F32),
                        pltpu.VMEM((4, FD, tb), F32)],
    )
    return pl.pallas_call(
        _attn_kernel,
        grid_spec=grid_spec,
        out_shape=jax.ShapeDtypeStruct((t, MIX), F32),
        compiler_params=pltpu.CompilerParams(dimension_semantics=("arbitrary",), vmem_limit_bytes=VMEM_LIMIT),
        name="prompt_attention",
    )(qi_tab, kj_tab, qd, qm, qf, kd, vdt, km, ckvt, kf, vft, lp["wuv"], gcol, scal)


L_MLA = 32
L_FOX = 48


def _decode_kernel(pp, nc, pt_ref, *refs):
    page_refs = refs[:4 * pp]
    (wd_ref, wn_ref, wr_ref, wf_ref, nd_ref, nm_ref, nf_ref, nl_ref,
     wuk_ref, gsel_ref, tri_ref, wuvt_ref, gcol_ref, scal_ref,
     od_ref, om_ref, of_ref,
     m_ref, l_ref, accd_ref, accm_ref, accf_ref, fc_ref) = refs[4 * pp:]
    c = pl.program_id(1)

    @pl.when(c == 0)
    def _():
        m_ref[...] = jnp.full(m_ref.shape, NEG, F32)
        l_ref[...] = jnp.zeros_like(l_ref)
        accd_ref[...] = jnp.zeros_like(accd_ref)
        accm_ref[...] = jnp.zeros_like(accm_ref)
        accf_ref[...] = jnp.zeros_like(accf_ref)
        fc_ref[...] = jnp.zeros_like(fc_ref)

    lane = lax.broadcasted_iota(jnp.int32, (PAGE, 128), 1)
    tri = tri_ref[...]
    wd = wd_ref[0]
    wn = wn_ref[0]
    wr = wr_ref[0]
    wf = wf_ref[0]

    def update(kd, mr, fr, lf, mask):
        k12 = kd[:, 0:128].astype(BF16)
        v = kd[:, 128:256]
        ckv = mr[:, 0:128]
        ckv_b = ckv.astype(BF16)
        kr = mr[:, 128:160].astype(BF16)
        fk = fr[:, 0:64].astype(BF16)
        fv = fr[:, 64:128]
        a = _dotf(k12, wd) + _dotf(kr, wr) + _dotf(fk, wf)
        kp = _dotf(ckv_b, wuk_ref[...])
        b = _dotf(kp.astype(BF16), wn)
        ssq = _dotf((kp * kp).astype(BF16), gsel_ref[...])
        s = a + lax.rsqrt(ssq + EPS) * b
        lexp = jnp.zeros((PAGE, 128), F32)
        for h in range(HEADS):
            lo_l = L_FOX + 4 * h
            lexp = jnp.where((lane >= lo_l) & (lane < lo_l + 4), lf[:, h:h + 1], lexp)
        hi, mid, lo = _split3(lexp)
        fp = fc_ref[...] + (_dotf(tri, hi) + _dotf(tri, mid) + _dotf(tri, lo))
        fc_ref[...] = fc_ref[...] + jnp.sum(lexp, axis=0, keepdims=True)
        s = s - LOG2E * fp
        if mask is not None:
            s = jnp.where(mask, s, NEG)
        m_prev = m_ref[...]
        m_new = jnp.maximum(m_prev, jnp.max(s, axis=0, keepdims=True))
        alpha = jnp.exp2(m_prev - m_new)
        p = jnp.exp2(s - m_new)
        l_ref[...] = alpha * l_ref[...] + jnp.sum(p, axis=0, keepdims=True)
        m_ref[...] = m_new
        pb = p.astype(BF16)
        accd_ref[...] = alpha * accd_ref[...] + _dotf(v.T.astype(BF16), pb)
        accm_ref[...] = alpha * accm_ref[...] + _dotf(ckv.T.astype(BF16), pb)
        accf_ref[...] = alpha * accf_ref[...] + _dotf(fr.T[64:128, :].astype(BF16), pb)

    for p in range(pp):
        d_ref, ml_ref, f_ref, lf_ref = page_refs[4 * p:4 * p + 4]
        update(d_ref[0, 0], ml_ref[0, 0], f_ref[0, 0], lf_ref[0, 0], None)

    @pl.when(c == nc - 1)
    def _():
        row = lax.broadcasted_iota(jnp.int32, (PAGE, 128), 0)
        mask = row <= (lane & 3)
        update(nd_ref[0], nm_ref[0], nf_ref[0], nl_ref[0][:, 0:HEADS], mask)
        lam = scal_ref[0]
        oml = scal_ref[1]
        linv = 1.0 / l_ref[...]
        od = accd_ref[...] * linv
        do = od - lam * pltpu.roll(od, 112, 1)
        do = (do * lax.rsqrt(jnp.mean(do * do, axis=0, keepdims=True) + EPS)) * gcol_ref[...] * oml
        od_ref[0] = do
        ctx = (accm_ref[...] * linv).astype(BF16)
        om_ref[0] = _dotf(wuvt_ref[...], ctx)
        of_ref[0] = accf_ref[...] * linv


def _decode_attention(layer, caches, page_table, q_tok, new_rows, lp, tri, *, pp):
    cache_diff, cache_mla, cache_fox, cache_logf = caches
    qd, qm, qf = q_tok
    b, n_pages = page_table.shape
    assert n_pages % pp == 0
    nc = n_pages // pp
    f32 = lambda a: a.astype(F32)
    qd5 = f32(qd[:b * 4]).reshape(b, 4, 2, HEADS, DQK)
    qd5 = jnp.transpose(qd5, (0, 2, 4, 3, 1)).reshape(b, 2, DQK, 16)
    wd = jnp.zeros((b, 128, 128), F32)
    wd = wd.at[:, 0:64, 0:16].set(qd5[:, 0]).at[:, 64:128, 16:32].set(qd5[:, 1]).astype(BF16)
    qn = f32(qm[:b * 4, 0:256]).reshape(b, 4, HEADS, NOPE) * lp["kn_gain"]
    qn = jnp.transpose(qn, (0, 2, 3, 1))
    wn = jnp.zeros((b, HEADS * NOPE, 128), F32)
    for h in range(HEADS):
        wn = wn.at[:, h * NOPE:(h + 1) * NOPE, L_MLA + 4 * h:L_MLA + 4 * h + 4].set(qn[:, h])
    wn = wn.astype(BF16)
    qr = jnp.transpose(f32(qm[:b * 4, 256:384]).reshape(b, 4, HEADS, ROPE), (0, 3, 2, 1)).reshape(b, ROPE, 16)
    wr = jnp.zeros((b, ROPE, 128), F32).at[:, :, L_MLA:L_MLA + 16].set(qr).astype(BF16)
    qf4 = jnp.transpose(f32(qf[:b * 4]).reshape(b, 4, HEADS, FD), (0, 3, 2, 1)).reshape(b, FD, 16)
    wf = jnp.zeros((b, FD, 128), F32).at[:, :, L_FOX:L_FOX + 16].set(qf4).astype(BF16)
    padk = lambda a: jnp.pad(a[:b * 4].reshape(b, 4, a.shape[-1]), ((0, 0), (0, PAGE - 4), (0, 0)))
    nd, nm, nf, nl = (padk(a) for a in new_rows)
    gsel = np.zeros((HEADS * NOPE, 128), np.float32)
    for h in range(HEADS):
        gsel[h * NOPE:(h + 1) * NOPE, L_MLA + 4 * h:L_MLA + 4 * h + 4] = 1.0 / NOPE
    gsel = jnp.asarray(gsel, BF16)
    wuvt = jnp.transpose(lp["wuv_raw"], (1, 2, 0)).reshape(HEADS * 64, KV_LORA).astype(BF16)
    scal = jnp.stack([lp["lam"], jnp.asarray(1.0 - lp["lam_init"], F32)]).astype(F32)
    gcol = lp["out_gain"].reshape(DV, 1).astype(F32)

    def page_spec(w, p):
        return pl.BlockSpec((1, 1, PAGE, w), lambda bi, ci, pt, _p=p: (layer, pt[bi * n_pages + ci * pp + _p], 0, 0))

    in_specs = []
    operands = []
    for p in range(pp):
        in_specs += [page_spec(DIFF_ROW, p), page_spec(MLA_ROW, p), page_spec(FOX_ROW, p), page_spec(HEADS, p)]
        operands += [cache_diff, cache_mla, cache_fox, cache_logf]
    seq = lambda r, w: pl.BlockSpec((1, r, w), lambda bi, ci, pt: (bi, 0, 0))
    cst = lambda shape: pl.BlockSpec(shape, lambda bi, ci, pt, _n=len(shape): (0,) * _n)
    in_specs += [seq(128, 128), seq(HEADS * NOPE, 128), seq(ROPE, 128), seq(FD, 128),
                 seq(PAGE, DIFF_ROW), seq(PAGE, MLA_ROW), seq(PAGE, FOX_ROW), seq(PAGE, 128),
                 cst((KV_LORA, HEADS * NOPE)), cst((HEADS * NOPE, 128)), cst((PAGE, PAGE)),
                 cst((HEADS * 64, KV_LORA)), cst((DV, 1)), pl.BlockSpec(memory_space=pltpu.SMEM)]
    operands += [wd, wn, wr, wf, nd, nm, nf, nl, lp["wuk_dense"], gsel, tri, wuvt, gcol, scal]
    grid_spec = pltpu.PrefetchScalarGridSpec(
        num_scalar_prefetch=1,
        grid=(b, nc),
        in_specs=in_specs,
        out_specs=[seq(DV, 128), seq(HEADS * 64, 128), seq(FD, 128)],
        scratch_shapes=[pltpu.VMEM((1, 128), F32), pltpu.VMEM((1, 128), F32), pltpu.VMEM((DV, 128), F32),
                        pltpu.VMEM((KV_LORA, 128), F32), pltpu.VMEM((FD, 128), F32), pltpu.VMEM((1, 128), F32)],
    )
    od, om, of = pl.pallas_call(
        functools.partial(_decode_kernel, pp, nc),
        grid_spec=grid_spec,
        out_shape=[jax.ShapeDtypeStruct((b, DV, 128), F32), jax.ShapeDtypeStruct((b, HEADS * 64, 128), F32),
                   jax.ShapeDtypeStruct((b, FD, 128), F32)],
        compiler_params=pltpu.CompilerParams(dimension_semantics=("arbitrary", "arbitrary"),
                                             vmem_limit_bytes=VMEM_LIMIT),
        name="decode_attention",
    )(page_table.reshape(-1), *operands)
    md = jnp.transpose(od[:, :, 0:16].reshape(b, DV, HEADS, 4), (0, 3, 2, 1)).reshape(b, 4, HEADS * DV)
    mm = jnp.stack([om[:, h * 64:(h + 1) * 64, L_MLA + 4 * h:L_MLA + 4 * h + 4] for h in range(HEADS)], axis=1)
    mm = jnp.transpose(mm, (0, 3, 1, 2)).reshape(b, 4, HEADS * 64)
    mf = jnp.transpose(of[:, :, L_FOX:L_FOX + 16].reshape(b, FD, HEADS, 4), (0, 3, 2, 1)).reshape(b, 4, HEADS * FD)
    return jnp.concatenate([md, mm, mf], axis=-1).reshape(b * 4, MIX)


def _round_up(n, m):
    return (n + m - 1) // m * m


def _largest_tile(n, options):
    for o in options:
        if n % o == 0:
            return o
    raise ValueError(f"no tile for {n}")


def kernel(x_prompt, x_sample, cache_diff, cache_mla, cache_fox_kv, cache_fox_logf, page_table, meta_tokens, norm_gain, w_in, w_out, diff_q_gain, diff_k_gain, diff_lambda, diff_out_gain, mla_q_a_gain, mla_w_uq, mla_kv_gain, mla_w_uk, mla_w_uv, mla_q_nope_gain, mla_q_rope_gain, mla_k_nope_gain, mla_k_rope_gain, fox_q_gain, fox_k_gain, fox_f_bias):
    assert x_prompt.shape[0] == 1
    depth = w_in.shape[0]
    seq = x_prompt.shape[1]
    t_real = N_META + seq
    tb = 512 if t_real >= 4096 else 128
    t_pad = _round_up(t_real, tb)
    tm_p = min(256, tb)
    b, t_new = x_sample.shape[0], x_sample.shape[1]
    assert t_new == 4
    n_pages = page_table.shape[1]
    past_len = n_pages * PAGE
    ts = b * t_new
    ts_pad = _round_up(ts, 16)
    tm_s = _largest_tile(ts_pad, (256, 128, 64, 32, 16))
    pp = _largest_tile(n_pages, (8, 4, 2, 1))

    xp = jnp.concatenate([meta_tokens.astype(F32), x_prompt[0], jnp.zeros((t_pad - t_real, D_MODEL), F32)], axis=0)
    pos_p = jnp.arange(t_pad, dtype=F32).reshape(t_pad, 1)
    xs = jnp.pad(x_sample.reshape(ts, D_MODEL), ((0, ts_pad - ts), (0, 0)))
    pos_s = (past_len + (jnp.arange(ts_pad) % t_new)).astype(F32).reshape(ts_pad, 1)

    inv64, inv32 = _inv_freq_rows()
    g64 = jnp.asarray(_block_diag_mean(256, 64), BF16)
    g32 = jnp.asarray(_block_diag_mean(128, 32), BF16)
    tri_p = jnp.asarray(np.tril(np.ones((tm_p, tm_p), np.float32)), BF16)
    tri_s = jnp.asarray(np.tril(np.ones((tm_s, tm_s), np.float32)), BF16)
    tri_pg = jnp.asarray(np.tril(np.ones((PAGE, PAGE), np.float32)), BF16)
    caches = (cache_diff, cache_mla, cache_fox_kv, cache_fox_logf)

    rows_p, rows_s = [], []
    for l in range(depth):
        lp = _layer_params(l, norm_gain, w_in, w_out, diff_q_gain, diff_k_gain, diff_lambda, diff_out_gain,
                           mla_q_a_gain, mla_w_uq, mla_kv_gain, mla_w_uk, mla_w_uv, mla_q_nope_gain,
                           mla_q_rope_gain, mla_k_nope_gain, mla_k_rope_gain, fox_q_gain, fox_k_gain, fox_f_bias)
        lp["wuv_raw"] = mla_w_uv[l]
        po = _proj(xp, pos_p, lp, (inv64, inv32, g64, g32, tri_p), prompt=True, tm=tm_p)
        ops, (drow, mrow, frow, logf, sg) = po[:9], po[9:]
        mixed = _prompt_attention(ops, lp, tb=tb)
        xp = _out_proj(xp, mixed, sg, lp["wout"], tm=tm_p)
        rows_p.append((drow[:t_real], mrow[:t_real], frow[:t_real], logf[:t_real, :HEADS]))

        so = _proj(xs, pos_s, lp, (inv64, inv32, g64, g32, tri_s), prompt=False, tm=tm_s)
        q_tok, (drs, mrs, frs, lfs, sgs) = so[:3], so[3:]
        mixed_s = _decode_attention(l, caches, page_table, q_tok, (drs, mrs, frs, lfs), lp, tri_pg, pp=pp)
        mixed_s = jnp.pad(mixed_s, ((0, ts_pad - ts), (0, 0)))
        xs = _out_proj(xs, mixed_s, sgs, lp["wout"], tm=tm_s)
        rows_s.append((drs[:ts], mrs[:ts], frs[:ts], lfs[:ts, :HEADS]))

    y_prompt = xp[N_META:t_real][None]
    y_sample = xs[:ts].reshape(b, t_new, D_MODEL)
    outs_p = [jnp.stack([r[k] for r in rows_p])[:, None] for k in range(4)]
    outs_s = [jnp.stack([r[k].reshape(b, t_new, -1) for r in rows_s]) for k in range(4)]
    return (y_prompt, y_sample, *outs_p, *outs_s)
```

```python
import functools
import math

import numpy as np
import jax
import jax.numpy as jnp
from jax import lax
from jax.experimental import pallas as pl
from jax.experimental.pallas import tpu as pltpu

F32 = jnp.float32
BF16 = jnp.bfloat16

D_MODEL = 1024
N_META = 16
PAGE = 128
ROPE_THETA = 10000.0
EPS = 1e-6
NEG = -1e30
LOG2E = 1.4426950408889634

HEADS = 4
DQK = 64
DV = 128
Q_LORA = 192
KV_LORA = 128
NOPE = 64
ROPE = 32
FD = 64
MIX = 1024
DIFF_ROW = 256
MLA_ROW = 160
FOX_ROW = 128

SC_D = DQK ** -0.5 * LOG2E
SC_M = (NOPE + ROPE) ** -0.5 * LOG2E
SC_F = FD ** -0.5 * LOG2E

O_DQ, O_DQR, O_DK, O_DKR, O_DV, O_CQ, O_CKV, O_KR, O_KRR, O_FQ, O_FKV, O_FZ, O_GATE, NP1 = (
    0, 512, 1024, 1152, 1280, 1408, 1664, 1792, 1920, 2048, 2304, 2432, 2560, 3584)

VMEM_LIMIT = 56 * 1024 * 1024


def _rot_idx(d):
    e = np.arange(d)
    return (e + d // 2) % d, np.where(e < d // 2, -1.0, 1.0).astype(np.float32)


P64, S64 = _rot_idx(64)
P32, S32 = _rot_idx(32)


def _in_cols():
    neg = lambda n: -np.ones(n, np.int64)
    dq = [(2 * h + j) * 64 + e for j in range(2) for h in range(HEADS) for e in range(64)]
    dqr = [(2 * h + j) * 64 + P64[e] for j in range(2) for h in range(HEADS) for e in range(64)]
    dk = [512 + g * 64 + e for g in range(2) for e in range(64)]
    dkr = [512 + g * 64 + P64[e] for g in range(2) for e in range(64)]
    parts = [
        np.array(dq), np.array(dqr), np.array(dk), np.array(dkr), 640 + np.arange(128),
        np.concatenate([768 + np.arange(192), neg(64)]), 960 + np.arange(128),
        np.concatenate([1088 + np.arange(32), neg(96)]), np.concatenate([1088 + P32, neg(96)]),
        1120 + np.arange(256), 1376 + np.arange(128), np.concatenate([1504 + np.arange(4), neg(124)]),
        1508 + np.arange(1024),
    ]
    cols = np.concatenate(parts)
    assert cols.shape[0] == NP1
    return cols


IN_COLS = _in_cols()


def _block_diag_mean(n, g):
    m = np.zeros((n, n), np.float32)
    for i in range(n // g):
        m[i * g:(i + 1) * g, i * g:(i + 1) * g] = 1.0 / g
    return m


def _row(v, n=None):
    v = jnp.asarray(v, F32).reshape(1, -1)
    if n is not None and v.shape[1] < n:
        v = jnp.pad(v, ((0, 0), (0, n - v.shape[1])))
    return v


def _layer_params(l, norm_gain, w_in, w_out, diff_q_gain, diff_k_gain, diff_lambda, diff_out_gain,
                  mla_q_a_gain, mla_w_uq, mla_kv_gain, mla_w_uk, mla_w_uv, mla_q_nope_gain, mla_q_rope_gain,
                  mla_k_nope_gain, mla_k_rope_gain, fox_q_gain, fox_k_gain, fox_f_bias):
    cols = jnp.asarray(np.maximum(IN_COLS, 0), jnp.int32)
    wp = jnp.where(jnp.asarray(IN_COLS >= 0)[None, :], jnp.take(w_in[l], cols, axis=1), 0.0).astype(BF16)
    wuq = mla_w_uq[l]
    nope = wuq[:, :, :NOPE].reshape(Q_LORA, HEADS * NOPE)
    rope = wuq[:, :, NOPE:]
    rope_rot = rope[:, :, P32].reshape(Q_LORA, HEADS * ROPE)
    rope = rope.reshape(Q_LORA, HEADS * ROPE)
    wuq2 = jnp.pad(jnp.concatenate([nope, rope, rope_rot], axis=1), ((0, 256 - Q_LORA), (0, 0))).astype(BF16)
    wuk = jnp.pad(mla_w_uk[l], ((0, 0), (0, 0), (0, 64))).reshape(KV_LORA, HEADS * 128).astype(BF16)
    wukt = jnp.transpose(mla_w_uk[l], (1, 2, 0)).reshape(HEADS * NOPE, KV_LORA)
    wukt_pad = jnp.pad(wukt, ((0, 0), (0, MLA_ROW - KV_LORA))).astype(BF16)
    wq = jnp.transpose(mla_w_uk[l] * mla_k_nope_gain[l][None, None, :], (1, 2, 0))
    wukt_bd = jnp.zeros((HEADS * NOPE, HEADS * KV_LORA), F32)
    for h in range(HEADS):
        wukt_bd = wukt_bd.at[h * NOPE:(h + 1) * NOPE, h * KV_LORA:(h + 1) * KV_LORA].set(wq[h])
    wukt_bd = wukt_bd.astype(BF16)
    wuv_all = mla_w_uv[l].reshape(KV_LORA, HEADS * 64).astype(BF16)
    wuv = jnp.zeros((HEADS, KV_LORA, HEADS * 64), F32)
    for h in range(HEADS):
        wuv = wuv.at[h, :, h * 64:(h + 1) * 64].set(mla_w_uv[l][:, h, :])
    wuv = wuv.astype(BF16)
    lam_init = 0.8 - 0.6 * math.exp(-0.3 * l)
    lq1, lk1, lq2, lk2 = diff_lambda[l, 0], diff_lambda[l, 1], diff_lambda[l, 2], diff_lambda[l, 3]
    lam = (jnp.exp(jnp.sum(lq1 * lk1, dtype=F32)) - jnp.exp(jnp.sum(lq2 * lk2, dtype=F32)) + lam_init)
    s64 = jnp.asarray(S64)
    s32 = jnp.asarray(S32)
    p64 = jnp.asarray(P64)
    p32 = jnp.asarray(P32)
    vec = dict(
        ng=_row(norm_gain[l]),
        dq_c=_row(jnp.tile(diff_q_gain[l], 8) * SC_D),
        dq_s=_row(jnp.tile(diff_q_gain[l][p64] * s64, 8) * SC_D),
        dk_c=_row(jnp.tile(diff_k_gain[l], 2)),
        dk_s=_row(jnp.tile(diff_k_gain[l][p64] * s64, 2)),
        cq_g=_row(mla_q_a_gain[l], 256),
        ckv_g=_row(mla_kv_gain[l]),
        kr_c=_row(mla_k_rope_gain[l], 128),
        kr_s=_row(mla_k_rope_gain[l][p32] * s32, 128),
        qn_g=_row(jnp.tile(mla_q_nope_gain[l], HEADS) * SC_M),
        qr_c=_row(jnp.tile(mla_q_rope_gain[l], HEADS) * SC_M),
        qr_s=_row(jnp.tile(mla_q_rope_gain[l][p32] * s32, HEADS) * SC_M),
        kn_g=_row(jnp.tile(jnp.pad(mla_k_nope_gain[l], (0, 64)), HEADS)),
        fq_g=_row(jnp.tile(fox_q_gain[l], HEADS) * SC_F),
        fk_g=_row(fox_k_gain[l], 128),
        fz_b=_row(fox_f_bias[l], 128),
    )
    return dict(wp=wp, wuq2=wuq2, wuk=wuk, wukt_pad=wukt_pad, wukt_bd=wukt_bd, wuv=wuv, wuv_all=wuv_all,
                wout=w_out[l].astype(BF16), vec=vec, lam=lam, lam_init=lam_init, out_gain=diff_out_gain[l])


VEC_NAMES = ("ng", "dq_c", "dq_s", "dk_c", "dk_s", "cq_g", "ckv_g", "kr_c", "kr_s", "qn_g", "qr_c", "qr_s",
             "kn_g", "fq_g", "fk_g", "fz_b")


def _inv_freq_rows():
    f64 = ROPE_THETA ** (-jnp.arange(0, 64, 2, dtype=F32) / 64)
    f32 = ROPE_THETA ** (-jnp.arange(0, 32, 2, dtype=F32) / 32)
    return jnp.tile(f64, 4).reshape(1, 128), jnp.tile(f32, 8).reshape(1, 128)


def _dotf(a, b):
    return jnp.dot(a, b, preferred_element_type=F32)


def _split3(x):
    hi = x.astype(BF16)
    r1 = x - hi.astype(F32)
    mid = r1.astype(BF16)
    lo = (r1 - mid.astype(F32)).astype(BF16)
    return hi, mid, lo


def _proj_kernel(prompt, x_ref, pos_ref, inv64_ref, inv32_ref, wp_ref, wuq_ref, wuk_ref, g64_ref, g32_ref, tri_ref,
                 ng, dq_c, dq_s, dk_c, dk_s, cq_g, ckv_g, kr_c, kr_s, qn_g, qr_c, qr_s, kn_g, fq_g, fk_g, fz_b,
                 *refs):
    if prompt:
        (qd_ref, qm_ref, qf_ref, kd_ref, vdt_ref, km_ref, ckvt_ref, kf_ref, vft_ref,
         drow_ref, mrow_ref, frow_ref, logf_ref, sg_ref, carry_ref) = refs
    else:
        (qd_ref, qt_ref, qr_ref, qf_ref, drow_ref, mrow_ref, frow_ref, logf_ref, sg_ref) = refs
    tm = x_ref.shape[0]

    x = x_ref[...]
    h = (x * lax.rsqrt(jnp.mean(x * x, axis=-1, keepdims=True) + EPS)) * ng[...]
    z = _dotf(h.astype(BF16), wp_ref[...])

    pos = pos_ref[...]
    a64 = pos * inv64_ref[...]
    a32 = pos * inv32_ref[...]
    c64, s64 = jnp.cos(a64), jnp.sin(a64)
    c32, s32 = jnp.cos(a32), jnp.sin(a32)
    g64 = g64_ref[...]
    g64h = g64_ref[0:128, 0:128]
    g32 = g32_ref[...]

    def inv_rms_g(v, g):
        return lax.rsqrt(_dotf((v * v).astype(BF16), g) + EPS)

    dq_parts = []
    for c in range(2):
        zq = z[:, O_DQ + c * 256:O_DQ + (c + 1) * 256]
        zr = z[:, O_DQR + c * 256:O_DQR + (c + 1) * 256]
        r = inv_rms_g(zq, g64)
        cc = jnp.concatenate([c64, c64], axis=1)
        ss = jnp.concatenate([s64, s64], axis=1)
        gc = dq_c[:, c * 256:(c + 1) * 256]
        gs = dq_s[:, c * 256:(c + 1) * 256]
        dq_parts.append(r * (zq * (gc * cc) + zr * (gs * ss)))
    zk = z[:, O_DK:O_DK + 128]
    zkr = z[:, O_DKR:O_DKR + 128]
    dk = inv_rms_g(zk, g64h) * (zk * (dk_c[...] * c64) + zkr * (dk_s[...] * s64))
    dv = z[:, O_DV:O_DV + 128]
    drow_ref[:, 0:128] = dk
    drow_ref[:, 128:256] = dv

    zc = z[:, O_CQ:O_CQ + 256]
    rc = lax.rsqrt(jnp.sum(zc * zc, axis=-1, keepdims=True) * (1.0 / Q_LORA) + EPS)
    mq = _dotf(((zc * rc) * cq_g[...]).astype(BF16), wuq_ref[...])
    qn_raw = mq[:, 0:256]
    qn = (qn_raw * inv_rms_g(qn_raw, g64)) * qn_g[...]
    qr_raw = mq[:, 256:384]
    qr_rot = mq[:, 384:512]
    qr = inv_rms_g(qr_raw, g32) * (qr_raw * (qr_c[...] * c32) + qr_rot * (qr_s[...] * s32))
    zv = z[:, O_CKV:O_CKV + 128]
    ckvn = (zv * lax.rsqrt(jnp.mean(zv * zv, axis=-1, keepdims=True) + EPS)) * ckv_g[...]
    zr0 = z[:, O_KR:O_KR + 128]
    zr1 = z[:, O_KRR:O_KRR + 128]
    rk = lax.rsqrt(jnp.sum(zr0 * zr0, axis=-1, keepdims=True) * (1.0 / ROPE) + EPS)
    krn = rk * (zr0 * (kr_c[...] * c32) + zr1 * (kr_s[...] * s32))
    mrow_ref[:, 0:128] = ckvn
    mrow_ref[:, 128:160] = krn[:, 0:32]

    zf = z[:, O_FQ:O_FQ + 256]
    fq = (zf * inv_rms_g(zf, g64)) * fq_g[...]
    fkv = z[:, O_FKV:O_FKV + 128]
    lane = lax.broadcasted_iota(jnp.int32, (tm, 128), 1)
    fkn = (fkv * inv_rms_g(fkv, g64h)) * fk_g[...]
    frow = jnp.where(lane < 64, fkn, fkv)
    frow_ref[...] = frow
    zz = z[:, O_FZ:O_FZ + 128] + fz_b[...]
    logf = jnp.minimum(zz, 0.0) - jnp.log1p(jnp.exp(-jnp.abs(zz)))
    logf = jnp.where((lane < HEADS) & (pos >= 0.0), logf, 0.0)
    logf_ref[...] = logf

    gate = z[:, O_GATE:O_GATE + MIX]
    sg_ref[...] = (gate * (1.0 / (1.0 + jnp.exp(-gate)))).astype(BF16)

    if not prompt:
        qd_ref[:, 0:256] = dq_parts[0].astype(BF16)
        qd_ref[:, 256:512] = dq_parts[1].astype(BF16)
        qt_ref[...] = _dotf(qn.astype(BF16), wuk_ref[...]).astype(BF16)
        qr_ref[...] = qr.astype(BF16)
        qf_ref[...] = fq.astype(BF16)
        return

    z64 = jnp.zeros((64, tm), BF16)
    for j in range(2):
        qt = dq_parts[j].T
        for hh in range(HEADS):
            blk = qt[hh * 64:(hh + 1) * 64, :].astype(BF16)
            qd_ref[j * HEADS + hh, j * 64:(j + 1) * 64, :] = blk
            qd_ref[j * HEADS + hh, (1 - j) * 64:(2 - j) * 64, :] = z64
    kd_ref[...] = dk.astype(BF16)
    vdt_ref[...] = dv.T.astype(BF16)

    qnt = qn.T
    qrt = qr.T
    for hh in range(HEADS):
        qm_ref[hh, 0:64, :] = qnt[hh * 64:(hh + 1) * 64, :].astype(BF16)
        qm_ref[hh, 64:96, :] = qrt[hh * 32:(hh + 1) * 32, :].astype(BF16)
        qm_ref[hh, 96:128, :] = jnp.zeros((32, tm), BF16)
    kp = _dotf(ckvn.astype(BF16), wuk_ref[...])
    kr_sh = pltpu.roll(krn, 64, 1)
    for hh in range(HEADS):
        kph = kp[:, hh * 128:(hh + 1) * 128]
        knh = (kph * inv_rms_g(kph, g64h)) * kn_g[:, hh * 128:(hh + 1) * 128]
        km_ref[hh] = (knh + kr_sh).astype(BF16)
    ckvt_ref[...] = ckvn.T.astype(BF16)

    @pl.when(pl.program_id(0) == 0)
    def _():
        carry_ref[...] = jnp.zeros_like(carry_ref)
    hi, mid, lo = _split3(logf)
    tri = tri_ref[...]
    fcum = carry_ref[0:1, :] + (_dotf(tri, hi) + _dotf(tri, mid) + _dotf(tri, lo))
    carry_ref[...] = carry_ref[...] + jnp.sum(logf, axis=0, keepdims=True)
    f2 = fcum * (-LOG2E)
    fh, fm, fl = _split3(f2)
    aug = fh.astype(F32) + pltpu.roll(fm.astype(F32), 4, 1) + pltpu.roll(fl.astype(F32), 8, 1)
    kf_ref[...] = (jnp.where(lane < 64, fkn, 0.0) + pltpu.roll(aug, 64, 1)).astype(BF16)
    frt = frow.T
    vft_ref[...] = frt[64:128, :].astype(BF16)
    fqt = fq.T
    rowi = lax.broadcasted_iota(jnp.int32, (64, tm), 0)
    for hh in range(HEADS):
        qf_ref[hh, 0:64, :] = fqt[hh * 64:(hh + 1) * 64, :].astype(BF16)
        pat = (rowi == hh) | (rowi == 4 + hh) | (rowi == 8 + hh)
        qf_ref[hh, 64:128, :] = jnp.where(pat, 1.0, 0.0).astype(BF16)


def _const_spec(shape):
    nd = len(shape)
    return pl.BlockSpec(shape, lambda i, _n=nd: (0,) * _n)


def _proj(x, pos, lp, consts, *, prompt, tm):
    t = x.shape[0]
    assert t % tm == 0
    n = t // tm
    inv64, inv32, g64, g32, tri = consts
    vecs = [lp["vec"][k] for k in VEC_NAMES]
    wuk = lp["wuk"] if prompt else lp["wukt_bd"]
    in_specs = [
        pl.BlockSpec((tm, D_MODEL), lambda i: (i, 0)),
        pl.BlockSpec((tm, 1), lambda i: (i, 0)),
        _const_spec(inv64.shape), _const_spec(inv32.shape),
        _const_spec(lp["wp"].shape), _const_spec(lp["wuq2"].shape), _const_spec(wuk.shape),
        _const_spec(g64.shape), _const_spec(g32.shape), _const_spec(tri.shape),
    ] + [_const_spec(v.shape) for v in vecs]
    tok = lambda w: pl.BlockSpec((tm, w), lambda i: (i, 0))
    feat = lambda r: pl.BlockSpec((r, tm), lambda i: (0, i))
    feat3 = lambda a, r: pl.BlockSpec((a, r, tm), lambda i: (0, 0, i))
    rows_shapes = [jax.ShapeDtypeStruct((t, DIFF_ROW), F32), jax.ShapeDtypeStruct((t, MLA_ROW), F32),
                   jax.ShapeDtypeStruct((t, FOX_ROW), F32), jax.ShapeDtypeStruct((t, 128), F32),
                   jax.ShapeDtypeStruct((t, MIX), BF16)]
    rows_specs = [tok(DIFF_ROW), tok(MLA_ROW), tok(FOX_ROW), tok(128), tok(MIX)]
    if prompt:
        out_shape = [jax.ShapeDtypeStruct((8, 128, t), BF16), jax.ShapeDtypeStruct((4, 128, t), BF16),
                     jax.ShapeDtypeStruct((4, 128, t), BF16), jax.ShapeDtypeStruct((t, 128), BF16),
                     jax.ShapeDtypeStruct((128, t), BF16), jax.ShapeDtypeStruct((HEADS, t, 128), BF16),
                     jax.ShapeDtypeStruct((128, t), BF16), jax.ShapeDtypeStruct((t, 128), BF16),
                     jax.ShapeDtypeStruct((64, t), BF16)] + rows_shapes
        out_specs = [feat3(8, 128), feat3(4, 128), feat3(4, 128), tok(128), feat(128),
                     pl.BlockSpec((HEADS, tm, 128), lambda i: (0, i, 0)), feat(128),
                     tok(128), feat(64)] + rows_specs
        scratch = [pltpu.VMEM((8, 128), F32)]
    else:
        out_shape = [jax.ShapeDtypeStruct((t, 512), BF16), jax.ShapeDtypeStruct((t, 512), BF16),
                     jax.ShapeDtypeStruct((t, 128), BF16), jax.ShapeDtypeStruct((t, 256), BF16)] + rows_shapes
        out_specs = [tok(512), tok(512), tok(128), tok(256)] + rows_specs
        scratch = []
    return pl.pallas_call(
        functools.partial(_proj_kernel, prompt),
        grid=(n,),
        in_specs=in_specs,
        out_specs=out_specs,
        out_shape=out_shape,
        scratch_shapes=scratch,
        compiler_params=pltpu.CompilerParams(dimension_semantics=("arbitrary",), vmem_limit_bytes=VMEM_LIMIT),
        name="proj_prompt" if prompt else "proj_sample",
    )(x, pos, inv64, inv32, lp["wp"], lp["wuq2"], wuk, g64, g32, tri, *vecs)


def _out_kernel(x_ref, mix_ref, sg_ref, w_ref, y_ref):
    g = (mix_ref[...] * sg_ref[...].astype(F32)).astype(BF16)
    y_ref[...] = x_ref[...] + _dotf(g, w_ref[...])


def _out_proj(x, mixed, sg, wout, *, tm):
    t = x.shape[0]
    assert t % tm == 0
    tok = lambda w: pl.BlockSpec((tm, w), lambda i: (i, 0))
    return pl.pallas_call(
        _out_kernel,
        grid=(t // tm,),
        in_specs=[tok(D_MODEL), tok(MIX), tok(MIX), _const_spec(wout.shape)],
        out_specs=tok(D_MODEL),
        out_shape=jax.ShapeDtypeStruct((t, D_MODEL), F32),
        compiler_params=pltpu.CompilerParams(dimension_semantics=("arbitrary",), vmem_limit_bytes=VMEM_LIMIT),
        name="out_proj",
    )(x, mixed, sg, wout)


def _attn_kernel(qi_tab, kj_tab, qd_ref, qm_ref, qf_ref, kd_ref, vdt_ref, km_ref, ckvt_ref, kf_ref, vft_ref,
                 wuv_ref, gcol_ref, scal_ref, mix_ref, *scratch):
    m_refs, l_refs, acc_refs, s_scr = scratch[0:4], scratch[4:8], scratch[8:12], scratch[12]
    tb = kd_ref.shape[0]
    i = pl.program_id(0)
    qi = qi_tab[i]
    kj = kj_tab[i]

    @pl.when(kj == 0)
    def _():
        for r in m_refs:
            r[...] = jnp.full(r.shape, NEG, F32)
        for r in l_refs + acc_refs:
            r[...] = jnp.zeros_like(r)

    def step(masked):
        if masked:
            mask = (lax.broadcasted_iota(jnp.int32, (tb, tb), 0) <= lax.broadcasted_iota(jnp.int32, (tb, tb), 1))

        k_ops = (lambda h: kd_ref[...], lambda h: kd_ref[...], lambda h: km_ref[h], lambda h: kf_ref[...])
        q_ops = (lambda h: qd_ref[h], lambda h: qd_ref[HEADS + h], lambda h: qm_ref[h], lambda h: qf_ref[h])
        v_ops = (vdt_ref, vdt_ref, ckvt_ref, vft_ref)

        def scores(k, h, slot):
            s = _dotf(k_ops[k](h), q_ops[k](h))
            if masked:
                s = jnp.where(mask, s, NEG)
            s_scr[slot] = s
            m_prev = m_refs[k][h]
            m_new = jnp.maximum(m_prev, jnp.max(s, axis=0, keepdims=True))
            m_refs[k][h] = m_new
            return m_new, jnp.exp2(m_prev - m_new)

        def weighted(k, h, slot, m_new, alpha):
            p = jnp.exp2(s_scr[slot] - m_new)
            l_refs[k][h] = alpha * l_refs[k][h] + jnp.sum(p, axis=0, keepdims=True)
            acc_refs[k][h] = alpha * acc_refs[k][h] + _dotf(v_ops[k][...], p.astype(BF16))

        prev = None
        for j, (h, k) in enumerate((h, k) for h in range(HEADS) for k in range(4)):
            cur = (k, h, j % 2) + scores(k, h, j % 2)
            if prev is not None:
                weighted(*prev)
            prev = cur
        weighted(*prev)

    @pl.when(kj < qi)
    def _():
        step(False)

    @pl.when(kj == qi)
    def _():
        step(True)
        lam = scal_ref[0]
        oml = scal_ref[1]
        gcol = gcol_ref[...]
        mo = jnp.zeros((tb, HEADS * 64), F32)
        fo = []
        for h in range(HEADS):
            o1 = acc_refs[0][h] * (1.0 / l_refs[0][h])
            o2 = acc_refs[1][h] * (1.0 / l_refs[1][h])
            do = o1 - lam * o2
            do = (do * lax.rsqrt(jnp.mean(do * do, axis=0, keepdims=True) + EPS)) * gcol * oml
            mix_ref[:, h * DV:(h + 1) * DV] = do.T
            ctx = (acc_refs[2][h] * (1.0 / l_refs[2][h])).T
            mo = mo + _dotf(ctx.astype(BF16), wuv_ref[h])
            fo.append(acc_refs[3][h] * (1.0 / l_refs[3][h]))
        mix_ref[:, 512:768] = mo
        mix_ref[:, 768:1024] = jnp.concatenate(fo, axis=0).T


def _pair_tables(nb):
    qi = np.concatenate([np.full(q + 1, q, np.int32) for q in range(nb)])
    kj = np.concatenate([np.arange(q + 1, dtype=np.int32) for q in range(nb)])
    return jnp.asarray(qi), jnp.asarray(kj)


def _prompt_attention(ops, lp, *, tb):
    qd, qm, qf, kd, vdt, km, ckvt, kf, vft = ops
    t = kd.shape[0]
    nb = t // tb
    qi_tab, kj_tab = _pair_tables(nb)
    scal = jnp.stack([lp["lam"], jnp.asarray(1.0 - lp["lam_init"], F32)]).astype(F32)
    gcol = lp["out_gain"].reshape(DV, 1).astype(F32)
    qspec = lambda a: pl.BlockSpec((a, 128, tb), lambda i, qt, kt: (0, 0, qt[i]))
    ktok = lambda w: pl.BlockSpec((tb, w), lambda i, qt, kt: (kt[i], 0))
    kfeat = lambda r: pl.BlockSpec((r, tb), lambda i, qt, kt: (0, kt[i]))
    grid_spec = pltpu.PrefetchScalarGridSpec(
        num_scalar_prefetch=2,
        grid=(int(qi_tab.shape[0]),),
        in_specs=[qspec(8), qspec(4), qspec(4), ktok(128), kfeat(128),
                  pl.BlockSpec((HEADS, tb, 128), lambda i, qt, kt: (0, kt[i], 0)), kfeat(128), ktok(128), kfeat(64),
                  pl.BlockSpec((HEADS, KV_LORA, 256), lambda i, qt, kt: (0, 0, 0)),
                  pl.BlockSpec((DV, 1), lambda i, qt, kt: (0, 0)),
                  pl.BlockSpec(memory_space=pltpu.SMEM)],
        out_specs=pl.BlockSpec((tb, MIX), lambda i, qt, kt: (qt[i], 0)),
        scratch_shapes=([pltpu.VMEM((HEADS, 1, tb), F32)] * 8
                        + [pltpu.VMEM((HEADS, DV, tb), F32), pltpu.VMEM((HEADS, DV, tb), F32),
                           pltpu.VMEM((HEADS, KV_LORA, tb), F32), pltpu.VMEM((HEADS, FD, tb), F32),
                           pltpu.VMEM((2, tb, tb), F32)]),
    )
    return pl.pallas_call(
        _attn_kernel,
        grid_spec=grid_spec,
        out_shape=jax.ShapeDtypeStruct((t, MIX), F32),
        compiler_params=pltpu.CompilerParams(dimension_semantics=("arbitrary",), vmem_limit_bytes=VMEM_LIMIT),
        name="prompt_attention",
    )(qi_tab, kj_tab, qd, qm, qf, kd, vdt, km, ckvt, kf, vft, lp["wuv"], gcol, scal)


def _decode_kernel(layer, pp, nc, nsteps, pt_ref, cd_hbm, cm_hbm, cf_hbm, cl_hbm,
                   qd_ref, qm_ref, qf_ref, nd_ref, nm_ref, nf_ref, nl_ref,
                   wukt_ref, u_ref, wuv_ref, grow_ref, scal_ref,
                   od_ref, om_ref, of_ref,
                   dbuf, mbuf, fbuf, lbuf, sem,
                   lhs_ref, d_s, mt_s, f_s, lf_s,
                   md_ref, ld_ref, mm_ref, lm_ref, mf_ref, lf_ref, accd_ref, accm_ref, accf_ref, fc_ref):
    c = pl.program_id(1)
    g = pl.program_id(0) * nc + c
    slot = g % 2
    hbm = (cd_hbm, cm_hbm, cf_hbm, cl_hbm)
    bufs = (dbuf, mbuf, fbuf, lbuf)

    def page_copy(k, page, sl, p):
        return pltpu.make_async_copy(hbm[k].at[layer, page], bufs[k].at[sl, p], sem.at[sl, k])

    def fetch(step, sl):
        for p in range(pp):
            page = pt_ref[step * pp + p]
            for k in range(4):
                page_copy(k, page, sl, p).start()

    @pl.when(g == 0)
    def _():
        fetch(0, 0)

    @pl.when(g + 1 < nsteps)
    def _():
        fetch(g + 1, 1 - slot)

    for p in range(pp):
        for k in range(4):
            page_copy(k, 0, slot, p).wait()

    @pl.when(c == 0)
    def _():
        for r in (md_ref, mm_ref, mf_ref):
            r[...] = jnp.full(r.shape, NEG, F32)
        for r in (ld_ref, lm_ref, lf_ref, accd_ref, accm_ref, accf_ref, fc_ref):
            r[...] = jnp.zeros_like(r)
        lhs_ref[0:HEADS * NOPE, :] = wukt_ref[...]
        lhs_ref[HEADS * NOPE:HEADS * NOPE + 32, :] = qm_ref[0]

    def nt(a, b):
        return lax.dot_general(a, b, (((1,), (1,)), ((), ())), preferred_element_type=F32)

    def soft(s, m_ref, l_ref):
        m_prev = m_ref[...]
        m_new = jnp.maximum(m_prev, jnp.max(s, axis=1, keepdims=True))
        alpha = jnp.exp2(m_prev - m_new)
        p = jnp.exp2(s - m_new)
        l_ref[...] = alpha * l_ref[...] + jnp.sum(p, axis=1, keepdims=True)
        m_ref[...] = m_new
        return alpha, p.astype(BF16)

    def update(d_b, mt_b, f_b, lf8, npg, masked):
        w = npg * PAGE
        mla = _dotf(lhs_ref[...], mt_b)
        kp = mla[0:HEADS * NOPE]
        kp2 = kp * kp
        rowh = lax.broadcasted_iota(jnp.int32, (16, w), 0) & 3
        r16 = None
        for h in range(HEADS):
            part = jnp.sum(kp2[h * NOPE:(h + 1) * NOPE].reshape(8, 8, w), axis=0)
            r_h = lax.rsqrt(jnp.sum(part, axis=0, keepdims=True) * (1.0 / NOPE) + EPS)
            r16 = jnp.broadcast_to(r_h, (16, w)) if r16 is None else jnp.where(rowh == h, r_h, r16)
        s_m = mla[256:272] * r16 + mla[272:288]
        s_d = nt(qd_ref[0], d_b)
        u = u_ref[...]
        hi, mid, lo = _split3(lf8)
        cum = _dotf(hi, u) + _dotf(mid, u) + _dotf(lo, u)
        carry = fc_ref[...]
        fparts = []
        for p in range(npg):
            cp = cum[p * 8:(p + 1) * 8]
            fparts.append(cp + carry)
            carry = carry + cp[:, PAGE - 1:PAGE]
        fc_ref[...] = carry
        f8 = fparts[0] if npg == 1 else jnp.concatenate(fparts, axis=1)
        s_f = nt(qf_ref[0], f_b) - LOG2E * jnp.concatenate([f8, f8], axis=0)
        if masked:
            key = lax.broadcasted_iota(jnp.int32, (32, w), 1)
            tq = (lax.broadcasted_iota(jnp.int32, (32, w), 0) >> 2) & 3
            vis = key <= tq
            s_d = jnp.where(vis, s_d, NEG)
            s_m = jnp.where(vis[0:16], s_m, NEG)
            s_f = jnp.where(vis[0:16], s_f, NEG)
        a_d, p_d = soft(s_d, md_ref, ld_ref)
        accd_ref[...] = a_d * accd_ref[...] + _dotf(p_d, d_b)
        a_m, p_m = soft(s_m, mm_ref, lm_ref)
        accm_ref[...] = a_m * accm_ref[...] + nt(p_m, mt_b[0:KV_LORA])
        a_f, p_f = soft(s_f, mf_ref, lf_ref)
        accf_ref[...] = a_f * accf_ref[...] + _dotf(p_f, f_b)

    for p in range(pp):
        d_s[p * PAGE:(p + 1) * PAGE, :] = dbuf[slot, p].astype(BF16)
        mt_s[:, p * PAGE:(p + 1) * PAGE] = mbuf[slot, p].astype(BF16)
        f_s[p * PAGE:(p + 1) * PAGE, :] = fbuf[slot, p].astype(BF16)
        lfp = lbuf[slot, p]
        lf_s[p * 8:p * 8 + HEADS, :] = lfp
        lf_s[p * 8 + HEADS:(p + 1) * 8, :] = lfp
    update(d_s[...], mt_s[...], f_s[...], lf_s[...], pp, False)

    @pl.when(c == nc - 1)
    def _():
        update(nd_ref[0].astype(BF16), nm_ref[0].astype(BF16), nf_ref[0].astype(BF16), nl_ref[0], 1, True)
        lam = scal_ref[0]
        oml = scal_ref[1]
        od = accd_ref[:, DV:2 * DV] * (1.0 / ld_ref[...])
        do = od[0:16] - lam * od[16:32]
        do = (do * lax.rsqrt(jnp.mean(do * do, axis=1, keepdims=True) + EPS)) * grow_ref[...] * oml
        od_ref[0] = do
        ctx = (accm_ref[...] * (1.0 / lm_ref[...])).astype(BF16)
        full = _dotf(ctx, wuv_ref[...])
        rh = lax.broadcasted_iota(jnp.int32, (16, 64), 0) & 3
        mo = full[:, 192:256]
        for h in range(HEADS - 1):
            mo = jnp.where(rh == h, full[:, h * 64:(h + 1) * 64], mo)
        om_ref[0] = mo
        of_ref[0] = (accf_ref[...] * (1.0 / lf_ref[...]))[:, FD:2 * FD]


def _decode_attention(layer, caches, page_table, q_tok, new_rows, lp, u_tri, *, pp):
    cache_diff, cache_mla_t, cache_fox, cache_logf_t = caches
    qd, qt, qr, qf = q_tok
    b, n_pages = page_table.shape
    assert n_pages % pp == 0
    nc = n_pages // pp
    ts = b * 4
    w = pp * PAGE
    qd5 = qd[:ts].reshape(b, 4, 2, HEADS, DQK)
    q1 = jnp.pad(qd5[:, :, 0], ((0, 0), (0, 0), (0, 0), (0, 192)))
    q2 = jnp.pad(qd5[:, :, 1], ((0, 0), (0, 0), (0, 0), (64, 128)))
    qdr = jnp.stack([q1, q2], axis=1).reshape(b, 32, DIFF_ROW)
    qmr = jnp.concatenate([
        jnp.pad(qt[:ts].reshape(b, 16, KV_LORA), ((0, 0), (0, 0), (0, ROPE))),
        jnp.pad(qr[:ts].reshape(b, 16, ROPE), ((0, 0), (0, 0), (KV_LORA, 0)))], axis=1)
    qfr = jnp.pad(qf[:ts].reshape(b, 16, FD), ((0, 0), (0, 0), (0, FD)))
    drs, mrs, frs, lfs = new_rows
    nd = jnp.pad(drs[:ts].reshape(b, 4, DIFF_ROW), ((0, 0), (0, PAGE - 4), (0, 0)))
    nm = jnp.pad(jnp.transpose(mrs[:ts].reshape(b, 4, MLA_ROW), (0, 2, 1)), ((0, 0), (0, 0), (0, PAGE - 4)))
    nf = jnp.pad(frs[:ts].reshape(b, 4, FOX_ROW), ((0, 0), (0, PAGE - 4), (0, 0)))
    nl = jnp.transpose(lfs[:ts, :HEADS].reshape(b, 4, HEADS), (0, 2, 1))
    nl = jnp.pad(jnp.concatenate([nl, nl], axis=1), ((0, 0), (0, 0), (0, PAGE - 4)))
    scal = jnp.stack([lp["lam"], jnp.asarray(1.0 - lp["lam_init"], F32)]).astype(F32)
    grow = lp["out_gain"].reshape(1, DV).astype(F32)

    in_specs = [pl.BlockSpec(memory_space=pl.ANY)] * 4
    operands = [cache_diff, cache_mla_t, cache_fox, cache_logf_t]
    seq = lambda r, wd: pl.BlockSpec((1, r, wd), lambda bi, ci, pt: (bi, 0, 0))
    cst = lambda shape: pl.BlockSpec(shape, lambda bi, ci, pt, _n=len(shape): (0,) * _n)
    in_specs += [seq(32, DIFF_ROW), seq(32, MLA_ROW), seq(16, FOX_ROW),
                 seq(PAGE, DIFF_ROW), seq(MLA_ROW, PAGE), seq(PAGE, FOX_ROW), seq(8, PAGE),
                 cst((HEADS * NOPE, MLA_ROW)), cst((PAGE, PAGE)), cst((KV_LORA, HEADS * 64)), cst((1, DV)),
                 pl.BlockSpec(memory_space=pltpu.SMEM)]
    operands += [qdr, qmr, qfr, nd, nm, nf, nl, lp["wukt_pad"], u_tri, lp["wuv_all"], grow, scal]
    col = lambda r: pltpu.VMEM((r, 1), F32)
    grid_spec = pltpu.PrefetchScalarGridSpec(
        num_scalar_prefetch=1,
        grid=(b, nc),
        in_specs=in_specs,
        out_specs=[seq(16, DV), seq(16, 64), seq(16, FD)],
        scratch_shapes=[pltpu.VMEM((2, pp, PAGE, DIFF_ROW), F32), pltpu.VMEM((2, pp, MLA_ROW, PAGE), F32),
                        pltpu.VMEM((2, pp, PAGE, FOX_ROW), F32), pltpu.VMEM((2, pp, HEADS, PAGE), F32),
                        pltpu.SemaphoreType.DMA((2, 4)),
                        pltpu.VMEM((HEADS * NOPE + 32, MLA_ROW), BF16),
                        pltpu.VMEM((w, DIFF_ROW), BF16), pltpu.VMEM((MLA_ROW, w), BF16),
                        pltpu.VMEM((w, FOX_ROW), BF16), pltpu.VMEM((pp * 8, PAGE), F32),
                        col(32), col(32), col(16), col(16), col(16), col(16),
                        pltpu.VMEM((32, DIFF_ROW), F32), pltpu.VMEM((16, KV_LORA), F32),
                        pltpu.VMEM((16, FOX_ROW), F32), col(8)],
    )
    od, om, of = pl.pallas_call(
        functools.partial(_decode_kernel, layer, pp, nc, b * nc),
        grid_spec=grid_spec,
        out_shape=[jax.ShapeDtypeStruct((b, 16, DV), F32), jax.ShapeDtypeStruct((b, 16, 64), F32),
                   jax.ShapeDtypeStruct((b, 16, FD), F32)],
        compiler_params=pltpu.CompilerParams(dimension_semantics=("arbitrary", "arbitrary"),
                                             vmem_limit_bytes=VMEM_LIMIT),
        name="decode_attention",
    )(page_table.reshape(-1), *operands)
    return jnp.concatenate([od.reshape(b, 4, HEADS * DV), om.reshape(b, 4, HEADS * 64),
                            of.reshape(b, 4, HEADS * FD)], axis=-1).reshape(ts, MIX)


def _round_up(n, m):
    return (n + m - 1) // m * m


def _largest_tile(n, options):
    for o in options:
        if n % o == 0:
            return o
    raise ValueError(f"no tile for {n}")


def kernel(x_prompt, x_sample, cache_diff, cache_mla, cache_fox_kv, cache_fox_logf, page_table, meta_tokens, norm_gain, w_in, w_out, diff_q_gain, diff_k_gain, diff_lambda, diff_out_gain, mla_q_a_gain, mla_w_uq, mla_kv_gain, mla_w_uk, mla_w_uv, mla_q_nope_gain, mla_q_rope_gain, mla_k_nope_gain, mla_k_rope_gain, fox_q_gain, fox_k_gain, fox_f_bias):
    assert x_prompt.shape[0] == 1
    depth = w_in.shape[0]
    seq = x_prompt.shape[1]
    t_real = N_META + seq
    tb = 512 if t_real >= 4096 else 128
    t_pad = _round_up(t_real, tb)
    tm_p = min(256, tb)
    b, t_new = x_sample.shape[0], x_sample.shape[1]
    assert t_new == 4
    n_pages = page_table.shape[1]
    past_len = n_pages * PAGE
    ts = b * t_new
    ts_pad = _round_up(ts, 16)
    tm_s = _largest_tile(ts_pad, (256, 128, 64, 32, 16))
    pp = _largest_tile(n_pages, (16, 8, 4, 2, 1))

    xp = jnp.concatenate([meta_tokens.astype(F32), x_prompt[0], jnp.zeros((t_pad - t_real, D_MODEL), F32)], axis=0)
    pos_p = jnp.arange(t_pad, dtype=F32).reshape(t_pad, 1)
    xs = jnp.pad(x_sample.reshape(ts, D_MODEL), ((0, ts_pad - ts), (0, 0)))
    pos_s = (past_len + (jnp.arange(ts_pad) % t_new)).astype(F32).reshape(ts_pad, 1)

    inv64, inv32 = _inv_freq_rows()
    g64 = jnp.asarray(_block_diag_mean(256, 64), BF16)
    g32 = jnp.asarray(_block_diag_mean(128, 32), BF16)
    tri_p = jnp.asarray(np.tril(np.ones((tm_p, tm_p), np.float32)), BF16)
    tri_s = jnp.asarray(np.tril(np.ones((tm_s, tm_s), np.float32)), BF16)
    u_pg = jnp.asarray(np.triu(np.ones((PAGE, PAGE), np.float32)), BF16)
    caches = (cache_diff, jnp.swapaxes(cache_mla, 2, 3), cache_fox_kv, jnp.swapaxes(cache_fox_logf, 2, 3))

    rows_p, rows_s = [], []
    for l in range(depth):
        lp = _layer_params(l, norm_gain, w_in, w_out, diff_q_gain, diff_k_gain, diff_lambda, diff_out_gain,
                           mla_q_a_gain, mla_w_uq, mla_kv_gain, mla_w_uk, mla_w_uv, mla_q_nope_gain,
                           mla_q_rope_gain, mla_k_nope_gain, mla_k_rope_gain, fox_q_gain, fox_k_gain, fox_f_bias)
        po = _proj(xp, pos_p, lp, (inv64, inv32, g64, g32, tri_p), prompt=True, tm=tm_p)
        ops, (drow, mrow, frow, logf, sg) = po[:9], po[9:]
        mixed = _prompt_attention(ops, lp, tb=tb)
        xp = _out_proj(xp, mixed, sg, lp["wout"], tm=tm_p)
        rows_p.append((drow[:t_real], mrow[:t_real], frow[:t_real], logf[:t_real, :HEADS]))

        so = _proj(xs, pos_s, lp, (inv64, inv32, g64, g32, tri_s), prompt=False, tm=tm_s)
        q_tok, (drs, mrs, frs, lfs, sgs) = so[:4], so[4:]
        mixed_s = _decode_attention(l, caches, page_table, q_tok, (drs, mrs, frs, lfs), lp, u_pg, pp=pp)
        mixed_s = jnp.pad(mixed_s, ((0, ts_pad - ts), (0, 0)))
        xs = _out_proj(xs, mixed_s, sgs, lp["wout"], tm=tm_s)
        rows_s.append((drs[:ts], mrs[:ts], frs[:ts], lfs[:ts, :HEADS]))

    y_prompt = xp[N_META:t_real][None]
    y_sample = xs[:ts].reshape(b, t_new, D_MODEL)
    outs_p = [jnp.stack([r[k] for r in rows_p])[:, None] for k in range(4)]
    outs_s = [jnp.stack([r[k].reshape(b, t_new, -1) for r in rows_s]) for k in range(4)]
    return (y_prompt, y_sample, *outs_p, *outs_s)
```

```python
import functools
import math

import numpy as np
import jax
import jax.numpy as jnp
from jax import lax
from jax.experimental import pallas as pl
from jax.experimental.pallas import tpu as pltpu

F32 = jnp.float32
BF16 = jnp.bfloat16

D_MODEL = 1024
N_META = 16
PAGE = 128
ROPE_THETA = 10000.0
EPS = 1e-6
NEG = -1e30
LOG2E = 1.4426950408889634

HEADS = 4
DQK = 64
DV = 128
Q_LORA = 192
KV_LORA = 128
NOPE = 64
ROPE = 32
FD = 64
MIX = 1024
DIFF_ROW = 256
MLA_ROW = 160
FOX_ROW = 128

SC_D = DQK ** -0.5 * LOG2E
SC_M = (NOPE + ROPE) ** -0.5 * LOG2E
SC_F = FD ** -0.5 * LOG2E

O_DQ, O_DQR, O_DK, O_DKR, O_DV, O_CQ, O_CKV, O_KR, O_KRR, O_FQ, O_FKV, O_FZ, O_GATE, NP1 = (
    0, 512, 1024, 1152, 1280, 1408, 1664, 1792, 1920, 2048, 2304, 2432, 2560, 3584)

VMEM_LIMIT = 56 * 1024 * 1024


def _rot_idx(d):
    e = np.arange(d)
    return (e + d // 2) % d, np.where(e < d // 2, -1.0, 1.0).astype(np.float32)


P64, S64 = _rot_idx(64)
P32, S32 = _rot_idx(32)


def _in_cols():
    neg = lambda n: -np.ones(n, np.int64)
    dq = [(2 * h + j) * 64 + e for j in range(2) for h in range(HEADS) for e in range(64)]
    dqr = [(2 * h + j) * 64 + P64[e] for j in range(2) for h in range(HEADS) for e in range(64)]
    dk = [512 + g * 64 + e for g in range(2) for e in range(64)]
    dkr = [512 + g * 64 + P64[e] for g in range(2) for e in range(64)]
    parts = [
        np.array(dq), np.array(dqr), np.array(dk), np.array(dkr), 640 + np.arange(128),
        np.concatenate([768 + np.arange(192), neg(64)]), 960 + np.arange(128),
        np.concatenate([1088 + np.arange(32), neg(96)]), np.concatenate([1088 + P32, neg(96)]),
        1120 + np.arange(256), 1376 + np.arange(128), np.concatenate([1504 + np.arange(4), neg(124)]),
        1508 + np.arange(1024),
    ]
    cols = np.concatenate(parts)
    assert cols.shape[0] == NP1
    return cols


IN_COLS = _in_cols()


def _block_diag_mean(n, g):
    m = np.zeros((n, n), np.float32)
    for i in range(n // g):
        m[i * g:(i + 1) * g, i * g:(i + 1) * g] = 1.0 / g
    return m


def _row(v, n=None):
    v = jnp.asarray(v, F32).reshape(1, -1)
    if n is not None and v.shape[1] < n:
        v = jnp.pad(v, ((0, 0), (0, n - v.shape[1])))
    return v


def _layer_params(l, norm_gain, w_in, w_out, diff_q_gain, diff_k_gain, diff_lambda, diff_out_gain,
                  mla_q_a_gain, mla_w_uq, mla_kv_gain, mla_w_uk, mla_w_uv, mla_q_nope_gain, mla_q_rope_gain,
                  mla_k_nope_gain, mla_k_rope_gain, fox_q_gain, fox_k_gain, fox_f_bias):
    cols = jnp.asarray(np.maximum(IN_COLS, 0), jnp.int32)
    wp = jnp.where(jnp.asarray(IN_COLS >= 0)[None, :], jnp.take(w_in[l], cols, axis=1), 0.0).astype(BF16)
    wuq = mla_w_uq[l]
    nope = wuq[:, :, :NOPE].reshape(Q_LORA, HEADS * NOPE)
    rope = wuq[:, :, NOPE:]
    rope_rot = rope[:, :, P32].reshape(Q_LORA, HEADS * ROPE)
    rope = rope.reshape(Q_LORA, HEADS * ROPE)
    wuq2 = jnp.pad(jnp.concatenate([nope, rope, rope_rot], axis=1), ((0, 256 - Q_LORA), (0, 0))).astype(BF16)
    wuk = jnp.pad(mla_w_uk[l], ((0, 0), (0, 0), (0, 64))).reshape(KV_LORA, HEADS * 128).astype(BF16)
    wukt = jnp.transpose(mla_w_uk[l], (1, 2, 0)).reshape(HEADS * NOPE, KV_LORA)
    wukt_pad = jnp.pad(wukt, ((0, 0), (0, MLA_ROW - KV_LORA))).astype(BF16)
    wq = jnp.transpose(mla_w_uk[l] * mla_k_nope_gain[l][None, None, :], (1, 2, 0))
    wukt_bd = jnp.zeros((HEADS * NOPE, HEADS * KV_LORA), F32)
    for h in range(HEADS):
        wukt_bd = wukt_bd.at[h * NOPE:(h + 1) * NOPE, h * KV_LORA:(h + 1) * KV_LORA].set(wq[h])
    wukt_bd = wukt_bd.astype(BF16)
    wuv_all = mla_w_uv[l].reshape(KV_LORA, HEADS * 64).astype(BF16)
    wuv = jnp.zeros((HEADS, KV_LORA, HEADS * 64), F32)
    for h in range(HEADS):
        wuv = wuv.at[h, :, h * 64:(h + 1) * 64].set(mla_w_uv[l][:, h, :])
    wuv = wuv.astype(BF16)
    lam_init = 0.8 - 0.6 * math.exp(-0.3 * l)
    lq1, lk1, lq2, lk2 = diff_lambda[l, 0], diff_lambda[l, 1], diff_lambda[l, 2], diff_lambda[l, 3]
    lam = (jnp.exp(jnp.sum(lq1 * lk1, dtype=F32)) - jnp.exp(jnp.sum(lq2 * lk2, dtype=F32)) + lam_init)
    s64 = jnp.asarray(S64)
    s32 = jnp.asarray(S32)
    p64 = jnp.asarray(P64)
    p32 = jnp.asarray(P32)
    vec = dict(
        ng=_row(norm_gain[l]),
        dq_c=_row(jnp.tile(diff_q_gain[l], 8) * SC_D),
        dq_s=_row(jnp.tile(diff_q_gain[l][p64] * s64, 8) * SC_D),
        dk_c=_row(jnp.tile(diff_k_gain[l], 2)),
        dk_s=_row(jnp.tile(diff_k_gain[l][p64] * s64, 2)),
        cq_g=_row(mla_q_a_gain[l], 256),
        ckv_g=_row(mla_kv_gain[l]),
        kr_c=_row(mla_k_rope_gain[l], 128),
        kr_s=_row(mla_k_rope_gain[l][p32] * s32, 128),
        qn_g=_row(jnp.tile(mla_q_nope_gain[l], HEADS) * SC_M),
        qr_c=_row(jnp.tile(mla_q_rope_gain[l], HEADS) * SC_M),
        qr_s=_row(jnp.tile(mla_q_rope_gain[l][p32] * s32, HEADS) * SC_M),
        kn_g=_row(jnp.tile(jnp.pad(mla_k_nope_gain[l], (0, 64)), HEADS)),
        fq_g=_row(jnp.tile(fox_q_gain[l], HEADS) * SC_F),
        fk_g=_row(fox_k_gain[l], 128),
        fz_b=_row(fox_f_bias[l], 128),
    )
    return dict(wp=wp, wuq2=wuq2, wuk=wuk, wukt_pad=wukt_pad, wukt_bd=wukt_bd, wuv=wuv, wuv_all=wuv_all,
                wout=w_out[l].astype(BF16), vec=vec, lam=lam, lam_init=lam_init, out_gain=diff_out_gain[l])


VEC_NAMES = ("ng", "dq_c", "dq_s", "dk_c", "dk_s", "cq_g", "ckv_g", "kr_c", "kr_s", "qn_g", "qr_c", "qr_s",
             "kn_g", "fq_g", "fk_g", "fz_b")


def _inv_freq_rows():
    f64 = ROPE_THETA ** (-jnp.arange(0, 64, 2, dtype=F32) / 64)
    f32 = ROPE_THETA ** (-jnp.arange(0, 32, 2, dtype=F32) / 32)
    return jnp.tile(f64, 4).reshape(1, 128), jnp.tile(f32, 8).reshape(1, 128)


def _dotf(a, b):
    return jnp.dot(a, b, preferred_element_type=F32)


def _split3(x):
    hi = x.astype(BF16)
    r1 = x - hi.astype(F32)
    mid = r1.astype(BF16)
    lo = (r1 - mid.astype(F32)).astype(BF16)
    return hi, mid, lo


def _proj_kernel(prompt, x_ref, pos_ref, inv64_ref, inv32_ref, wp_ref, wuq_ref, wuk_ref, g64_ref, g32_ref, tri_ref,
                 ng, dq_c, dq_s, dk_c, dk_s, cq_g, ckv_g, kr_c, kr_s, qn_g, qr_c, qr_s, kn_g, fq_g, fk_g, fz_b,
                 *refs):
    if prompt:
        (qd_ref, qm_ref, qf_ref, kd_ref, vdt_ref, km_ref, ckvt_ref, kf_ref, vft_ref,
         drow_ref, mrow_ref, frow_ref, logf_ref, sg_ref, carry_ref) = refs
    else:
        (qd_ref, qt_ref, qr_ref, qf_ref, drow_ref, mrow_ref, frow_ref, logf_ref, sg_ref) = refs
    tm = x_ref.shape[0]

    x = x_ref[...]
    h = (x * lax.rsqrt(jnp.mean(x * x, axis=-1, keepdims=True) + EPS)) * ng[...]
    z = _dotf(h.astype(BF16), wp_ref[...])

    pos = pos_ref[...]
    a64 = pos * inv64_ref[...]
    a32 = pos * inv32_ref[...]
    c64, s64 = jnp.cos(a64), jnp.sin(a64)
    c32, s32 = jnp.cos(a32), jnp.sin(a32)
    g64 = g64_ref[...]
    g64h = g64_ref[0:128, 0:128]
    g32 = g32_ref[...]

    def inv_rms_g(v, g):
        return lax.rsqrt(_dotf((v * v).astype(BF16), g) + EPS)

    dq_parts = []
    for c in range(2):
        zq = z[:, O_DQ + c * 256:O_DQ + (c + 1) * 256]
        zr = z[:, O_DQR + c * 256:O_DQR + (c + 1) * 256]
        r = inv_rms_g(zq, g64)
        cc = jnp.concatenate([c64, c64], axis=1)
        ss = jnp.concatenate([s64, s64], axis=1)
        gc = dq_c[:, c * 256:(c + 1) * 256]
        gs = dq_s[:, c * 256:(c + 1) * 256]
        dq_parts.append(r * (zq * (gc * cc) + zr * (gs * ss)))
    zk = z[:, O_DK:O_DK + 128]
    zkr = z[:, O_DKR:O_DKR + 128]
    dk = inv_rms_g(zk, g64h) * (zk * (dk_c[...] * c64) + zkr * (dk_s[...] * s64))
    dv = z[:, O_DV:O_DV + 128]
    drow_ref[:, 0:128] = dk
    drow_ref[:, 128:256] = dv

    zc = z[:, O_CQ:O_CQ + 256]
    rc = lax.rsqrt(jnp.sum(zc * zc, axis=-1, keepdims=True) * (1.0 / Q_LORA) + EPS)
    mq = _dotf(((zc * rc) * cq_g[...]).astype(BF16), wuq_ref[...])
    qn_raw = mq[:, 0:256]
    qn = (qn_raw * inv_rms_g(qn_raw, g64)) * qn_g[...]
    qr_raw = mq[:, 256:384]
    qr_rot = mq[:, 384:512]
    qr = inv_rms_g(qr_raw, g32) * (qr_raw * (qr_c[...] * c32) + qr_rot * (qr_s[...] * s32))
    zv = z[:, O_CKV:O_CKV + 128]
    ckvn = (zv * lax.rsqrt(jnp.mean(zv * zv, axis=-1, keepdims=True) + EPS)) * ckv_g[...]
    zr0 = z[:, O_KR:O_KR + 128]
    zr1 = z[:, O_KRR:O_KRR + 128]
    rk = lax.rsqrt(jnp.sum(zr0 * zr0, axis=-1, keepdims=True) * (1.0 / ROPE) + EPS)
    krn = rk * (zr0 * (kr_c[...] * c32) + zr1 * (kr_s[...] * s32))
    mrow_ref[:, 0:128] = ckvn
    mrow_ref[:, 128:160] = krn[:, 0:32]

    zf = z[:, O_FQ:O_FQ + 256]
    fq = (zf * inv_rms_g(zf, g64)) * fq_g[...]
    fkv = z[:, O_FKV:O_FKV + 128]
    lane = lax.broadcasted_iota(jnp.int32, (tm, 128), 1)
    fkn = (fkv * inv_rms_g(fkv, g64h)) * fk_g[...]
    frow = jnp.where(lane < 64, fkn, fkv)
    frow_ref[...] = frow
    zz = z[:, O_FZ:O_FZ + 128] + fz_b[...]
    logf = jnp.minimum(zz, 0.0) - jnp.log1p(jnp.exp(-jnp.abs(zz)))
    logf = jnp.where((lane < HEADS) & (pos >= 0.0), logf, 0.0)
    logf_ref[...] = logf

    gate = z[:, O_GATE:O_GATE + MIX]
    sg_ref[...] = (gate * (1.0 / (1.0 + jnp.exp(-gate)))).astype(BF16)

    if not prompt:
        qd_ref[:, 0:256] = dq_parts[0].astype(BF16)
        qd_ref[:, 256:512] = dq_parts[1].astype(BF16)
        qt_ref[...] = _dotf(qn.astype(BF16), wuk_ref[...]).astype(BF16)
        qr_ref[...] = qr.astype(BF16)
        qf_ref[...] = fq.astype(BF16)
        return

    z64 = jnp.zeros((64, tm), BF16)
    for j in range(2):
        qt = dq_parts[j].T
        for hh in range(HEADS):
            blk = qt[hh * 64:(hh + 1) * 64, :].astype(BF16)
            qd_ref[j * HEADS + hh, j * 64:(j + 1) * 64, :] = blk
            qd_ref[j * HEADS + hh, (1 - j) * 64:(2 - j) * 64, :] = z64
    kd_ref[...] = dk.astype(BF16)
    vdt_ref[...] = dv.T.astype(BF16)

    qnt = qn.T
    qrt = qr.T
    for hh in range(HEADS):
        qm_ref[hh, 0:64, :] = qnt[hh * 64:(hh + 1) * 64, :].astype(BF16)
        qm_ref[hh, 64:96, :] = qrt[hh * 32:(hh + 1) * 32, :].astype(BF16)
        qm_ref[hh, 96:128, :] = jnp.zeros((32, tm), BF16)
    kp = _dotf(ckvn.astype(BF16), wuk_ref[...])
    kr_sh = pltpu.roll(krn, 64, 1)
    for hh in range(HEADS):
        kph = kp[:, hh * 128:(hh + 1) * 128]
        knh = (kph * inv_rms_g(kph, g64h)) * kn_g[:, hh * 128:(hh + 1) * 128]
        km_ref[hh] = (knh + kr_sh).astype(BF16)
    ckvt_ref[...] = ckvn.T.astype(BF16)

    @pl.when(pl.program_id(0) == 0)
    def _():
        carry_ref[...] = jnp.zeros_like(carry_ref)
    hi, mid, lo = _split3(logf)
    tri = tri_ref[...]
    fcum = carry_ref[0:1, :] + (_dotf(tri, hi) + _dotf(tri, mid) + _dotf(tri, lo))
    carry_ref[...] = carry_ref[...] + jnp.sum(logf, axis=0, keepdims=True)
    f2 = fcum * (-LOG2E)
    fh, fm, fl = _split3(f2)
    aug = fh.astype(F32) + pltpu.roll(fm.astype(F32), 4, 1) + pltpu.roll(fl.astype(F32), 8, 1)
    kf_ref[...] = (jnp.where(lane < 64, fkn, 0.0) + pltpu.roll(aug, 64, 1)).astype(BF16)
    frt = frow.T
    vft_ref[...] = frt[64:128, :].astype(BF16)
    fqt = fq.T
    rowi = lax.broadcasted_iota(jnp.int32, (64, tm), 0)
    for hh in range(HEADS):
        qf_ref[hh, 0:64, :] = fqt[hh * 64:(hh + 1) * 64, :].astype(BF16)
        pat = (rowi == hh) | (rowi == 4 + hh) | (rowi == 8 + hh)
        qf_ref[hh, 64:128, :] = jnp.where(pat, 1.0, 0.0).astype(BF16)


def _const_spec(shape):
    nd = len(shape)
    return pl.BlockSpec(shape, lambda i, _n=nd: (0,) * _n)


def _proj(x, pos, lp, consts, *, prompt, tm):
    t = x.shape[0]
    assert t % tm == 0
    n = t // tm
    inv64, inv32, g64, g32, tri = consts
    vecs = [lp["vec"][k] for k in VEC_NAMES]
    wuk = lp["wuk"] if prompt else lp["wukt_bd"]
    in_specs = [
        pl.BlockSpec((tm, D_MODEL), lambda i: (i, 0)),
        pl.BlockSpec((tm, 1), lambda i: (i, 0)),
        _const_spec(inv64.shape), _const_spec(inv32.shape),
        _const_spec(lp["wp"].shape), _const_spec(lp["wuq2"].shape), _const_spec(wuk.shape),
        _const_spec(g64.shape), _const_spec(g32.shape), _const_spec(tri.shape),
    ] + [_const_spec(v.shape) for v in vecs]
    tok = lambda w: pl.BlockSpec((tm, w), lambda i: (i, 0))
    feat = lambda r: pl.BlockSpec((r, tm), lambda i: (0, i))
    feat3 = lambda a, r: pl.BlockSpec((a, r, tm), lambda i: (0, 0, i))
    rows_shapes = [jax.ShapeDtypeStruct((t, DIFF_ROW), F32), jax.ShapeDtypeStruct((t, MLA_ROW), F32),
                   jax.ShapeDtypeStruct((t, FOX_ROW), F32), jax.ShapeDtypeStruct((t, 128), F32),
                   jax.ShapeDtypeStruct((t, MIX), BF16)]
    rows_specs = [tok(DIFF_ROW), tok(MLA_ROW), tok(FOX_ROW), tok(128), tok(MIX)]
    if prompt:
        out_shape = [jax.ShapeDtypeStruct((8, 128, t), BF16), jax.ShapeDtypeStruct((4, 128, t), BF16),
                     jax.ShapeDtypeStruct((4, 128, t), BF16), jax.ShapeDtypeStruct((t, 128), BF16),
                     jax.ShapeDtypeStruct((128, t), BF16), jax.ShapeDtypeStruct((HEADS, t, 128), BF16),
                     jax.ShapeDtypeStruct((128, t), BF16), jax.ShapeDtypeStruct((t, 128), BF16),
                     jax.ShapeDtypeStruct((64, t), BF16)] + rows_shapes
        out_specs = [feat3(8, 128), feat3(4, 128), feat3(4, 128), tok(128), feat(128),
                     pl.BlockSpec((HEADS, tm, 128), lambda i: (0, i, 0)), feat(128),
                     tok(128), feat(64)] + rows_specs
        scratch = [pltpu.VMEM((8, 128), F32)]
    else:
        out_shape = [jax.ShapeDtypeStruct((t, 512), BF16), jax.ShapeDtypeStruct((t, 512), BF16),
                     jax.ShapeDtypeStruct((t, 128), BF16), jax.ShapeDtypeStruct((t, 256), BF16)] + rows_shapes
        out_specs = [tok(512), tok(512), tok(128), tok(256)] + rows_specs
        scratch = []
    return pl.pallas_call(
        functools.partial(_proj_kernel, prompt),
        grid=(n,),
        in_specs=in_specs,
        out_specs=out_specs,
        out_shape=out_shape,
        scratch_shapes=scratch,
        compiler_params=pltpu.CompilerParams(dimension_semantics=("arbitrary",), vmem_limit_bytes=VMEM_LIMIT),
        name="proj_prompt" if prompt else "proj_sample",
    )(x, pos, inv64, inv32, lp["wp"], lp["wuq2"], wuk, g64, g32, tri, *vecs)


def _out_kernel(x_ref, mix_ref, sg_ref, w_ref, y_ref):
    g = (mix_ref[...] * sg_ref[...].astype(F32)).astype(BF16)
    y_ref[...] = x_ref[...] + _dotf(g, w_ref[...])


def _out_proj(x, mixed, sg, wout, *, tm):
    t = x.shape[0]
    assert t % tm == 0
    tok = lambda w: pl.BlockSpec((tm, w), lambda i: (i, 0))
    return pl.pallas_call(
        _out_kernel,
        grid=(t // tm,),
        in_specs=[tok(D_MODEL), tok(MIX), tok(MIX), _const_spec(wout.shape)],
        out_specs=tok(D_MODEL),
        out_shape=jax.ShapeDtypeStruct((t, D_MODEL), F32),
        compiler_params=pltpu.CompilerParams(dimension_semantics=("arbitrary",), vmem_limit_bytes=VMEM_LIMIT),
        name="out_proj",
    )(x, mixed, sg, wout)


def _attn_kernel(qi_tab, kj_tab, qd_ref, qm_ref, qf_ref, kd_ref, vdt_ref, km_ref, ckvt_ref, kf_ref, vft_ref,
                 wuv_ref, gcol_ref, scal_ref, mix_ref, *scratch):
    m_refs, l_refs, acc_refs, s_scr = scratch[0:4], scratch[4:8], scratch[8:12], scratch[12]
    tb = kd_ref.shape[0]
    i = pl.program_id(0)
    qi = qi_tab[i]
    kj = kj_tab[i]

    @pl.when(kj == 0)
    def _():
        for r in m_refs:
            r[...] = jnp.full(r.shape, NEG, F32)
        for r in l_refs + acc_refs:
            r[...] = jnp.zeros_like(r)

    def step(masked):
        if masked:
            mask = (lax.broadcasted_iota(jnp.int32, (tb, tb), 0) <= lax.broadcasted_iota(jnp.int32, (tb, tb), 1))

        k_ops = (lambda h: kd_ref[...], lambda h: kd_ref[...], lambda h: km_ref[h], lambda h: kf_ref[...])
        q_ops = (lambda h: qd_ref[h], lambda h: qd_ref[HEADS + h], lambda h: qm_ref[h], lambda h: qf_ref[h])
        v_ops = (vdt_ref, vdt_ref, ckvt_ref, vft_ref)

        def scores(k, h, slot):
            s = _dotf(k_ops[k](h), q_ops[k](h))
            if masked:
                s = jnp.where(mask, s, NEG)
            s_scr[slot] = s
            m_prev = m_refs[k][h]
            m_new = jnp.maximum(m_prev, jnp.max(s, axis=0, keepdims=True))
            m_refs[k][h] = m_new
            return m_new, jnp.exp2(m_prev - m_new)

        def weighted(k, h, slot, m_new, alpha):
            p = jnp.exp2(s_scr[slot] - m_new)
            l_refs[k][h] = alpha * l_refs[k][h] + jnp.sum(p, axis=0, keepdims=True)
            acc_refs[k][h] = alpha * acc_refs[k][h] + _dotf(v_ops[k][...], p.astype(BF16))

        prev = None
        for j, (h, k) in enumerate((h, k) for h in range(HEADS) for k in range(4)):
            cur = (k, h, j % 2) + scores(k, h, j % 2)
            if prev is not None:
                weighted(*prev)
            prev = cur
        weighted(*prev)

    @pl.when(kj < qi)
    def _():
        step(False)

    @pl.when(kj == qi)
    def _():
        step(True)
        lam = scal_ref[0]
        oml = scal_ref[1]
        gcol = gcol_ref[...]
        mo = jnp.zeros((tb, HEADS * 64), F32)
        fo = []
        for h in range(HEADS):
            o1 = acc_refs[0][h] * (1.0 / l_refs[0][h])
            o2 = acc_refs[1][h] * (1.0 / l_refs[1][h])
            do = o1 - lam * o2
            do = (do * lax.rsqrt(jnp.mean(do * do, axis=0, keepdims=True) + EPS)) * gcol * oml
            mix_ref[:, h * DV:(h + 1) * DV] = do.T
            ctx = (acc_refs[2][h] * (1.0 / l_refs[2][h])).T
            mo = mo + _dotf(ctx.astype(BF16), wuv_ref[h])
            fo.append(acc_refs[3][h] * (1.0 / l_refs[3][h]))
        mix_ref[:, 512:768] = mo
        mix_ref[:, 768:1024] = jnp.concatenate(fo, axis=0).T


def _pair_tables(nb):
    qi = np.concatenate([np.full(q + 1, q, np.int32) for q in range(nb)])
    kj = np.concatenate([np.arange(q + 1, dtype=np.int32) for q in range(nb)])
    return jnp.asarray(qi), jnp.asarray(kj)


def _prompt_attention(ops, lp, *, tb):
    qd, qm, qf, kd, vdt, km, ckvt, kf, vft = ops
    t = kd.shape[0]
    nb = t // tb
    qi_tab, kj_tab = _pair_tables(nb)
    scal = jnp.stack([lp["lam"], jnp.asarray(1.0 - lp["lam_init"], F32)]).astype(F32)
    gcol = lp["out_gain"].reshape(DV, 1).astype(F32)
    qspec = lambda a: pl.BlockSpec((a, 128, tb), lambda i, qt, kt: (0, 0, qt[i]))
    ktok = lambda w: pl.BlockSpec((tb, w), lambda i, qt, kt: (kt[i], 0))
    kfeat = lambda r: pl.BlockSpec((r, tb), lambda i, qt, kt: (0, kt[i]))
    grid_spec = pltpu.PrefetchScalarGridSpec(
        num_scalar_prefetch=2,
        grid=(int(qi_tab.shape[0]),),
        in_specs=[qspec(8), qspec(4), qspec(4), ktok(128), kfeat(128),
                  pl.BlockSpec((HEADS, tb, 128), lambda i, qt, kt: (0, kt[i], 0)), kfeat(128), ktok(128), kfeat(64),
                  pl.BlockSpec((HEADS, KV_LORA, 256), lambda i, qt, kt: (0, 0, 0)),
                  pl.BlockSpec((DV, 1), lambda i, qt, kt: (0, 0)),
                  pl.BlockSpec(memory_space=pltpu.SMEM)],
        out_specs=pl.BlockSpec((tb, MIX), lambda i, qt, kt: (qt[i], 0)),
        scratch_shapes=([pltpu.VMEM((HEADS, 1, tb), F32)] * 8
                        + [pltpu.VMEM((HEADS, DV, tb), F32), pltpu.VMEM((HEADS, DV, tb), F32),
                           pltpu.VMEM((HEADS, KV_LORA, tb), F32), pltpu.VMEM((HEADS, FD, tb), F32),
                           pltpu.VMEM((2, tb, tb), F32)]),
    )
    return pl.pallas_call(
        _attn_kernel,
        grid_spec=grid_spec,
        out_shape=jax.ShapeDtypeStruct((t, MIX), F32),
        compiler_params=pltpu.CompilerParams(dimension_semantics=("arbitrary",), vmem_limit_bytes=VMEM_LIMIT),
        name="prompt_attention",
    )(qi_tab, kj_tab, qd, qm, qf, kd, vdt, km, ckvt, kf, vft, lp["wuv"], gcol, scal)


NSEQ = 1


def _decode_kernel(layer, pp, nc, n_pages, nsteps, pt_ref, cd_hbm, cm_hbm, cf_hbm, cl_hbm,
                   qd_all, qm_all, qf_all, nd_all, nm_all, nf_all, nl_all,
                   wukt_ref, u_ref, wuv_ref, grow_ref, scal_ref,
                   od_all, om_all, of_all,
                   dbuf, mbuf, fbuf, lbuf, sem,
                   lhs_all, d_all, mt_all, f_all, lf_all,
                   md_all, ld_all, mm_all, lm_all, mf_all, lff_all, accd_all, accm_all, accf_all, fc_all):
    c = pl.program_id(1)
    g = pl.program_id(0) * nc + c
    slot = g % 2
    hbm = (cd_hbm, cm_hbm, cf_hbm, cl_hbm)
    bufs = (dbuf, mbuf, fbuf, lbuf)

    def page_copy(k, page, sl, p):
        return pltpu.make_async_copy(hbm[k].at[layer, page], bufs[k].at[sl, p], sem.at[sl, k])

    def fetch(step, sl):
        base = (step // nc) * (NSEQ * n_pages) + (step % nc) * pp
        for sq in range(NSEQ):
            for p in range(pp):
                page = pt_ref[base + sq * n_pages + p]
                for k in range(4):
                    page_copy(k, page, sl, sq * pp + p).start()

    def wait_slot(sl):
        for p in range(NSEQ * pp):
            for k in range(4):
                page_copy(k, 0, sl, p).wait()

    @pl.when(g == 0)
    def _():
        fetch(0, 0)

    @pl.when(g + 1 < nsteps)
    def _():
        fetch(g + 1, 1 - slot)

    wait_slot(slot)

    @pl.when(c == 0)
    def _():
        for r in (md_all, mm_all, mf_all):
            r[...] = jnp.full(r.shape, NEG, F32)
        for r in (ld_all, lm_all, lff_all, accd_all, accm_all, accf_all, fc_all):
            r[...] = jnp.zeros_like(r)
        for sq in range(NSEQ):
            lhs_all[sq, 0:HEADS * NOPE, :] = wukt_ref[...]
            lhs_all[sq, HEADS * NOPE:HEADS * NOPE + 32, :] = qm_all[sq]

    def nt(a, b):
        return lax.dot_general(a, b, (((1,), (1,)), ((), ())), preferred_element_type=F32)

    def soft(s, m_ref, l_ref):
        m_prev = m_ref[...]
        m_new = jnp.maximum(m_prev, jnp.max(s, axis=1, keepdims=True))
        alpha = jnp.exp2(m_prev - m_new)
        p = jnp.exp2(s - m_new)
        l_ref[...] = alpha * l_ref[...] + jnp.sum(p, axis=1, keepdims=True)
        m_ref[...] = m_new
        return alpha, p.astype(BF16)

    def update(sq, d_b, mt_b, f_b, lf8, npg, masked):
        w = npg * PAGE
        mla = _dotf(lhs_all[sq], mt_b)
        kp = mla[0:HEADS * NOPE]
        kp2 = kp * kp
        rowh = lax.broadcasted_iota(jnp.int32, (16, w), 0) & 3
        r16 = None
        for h in range(HEADS):
            part = jnp.sum(kp2[h * NOPE:(h + 1) * NOPE].reshape(8, 8, w), axis=0)
            r_h = lax.rsqrt(jnp.sum(part, axis=0, keepdims=True) * (1.0 / NOPE) + EPS)
            r16 = jnp.broadcast_to(r_h, (16, w)) if r16 is None else jnp.where(rowh == h, r_h, r16)
        s_m = mla[256:272] * r16 + mla[272:288]
        s_d = nt(qd_all[sq], d_b[:, 0:2 * DQK])
        u = u_ref[...]
        hi, mid, lo = _split3(lf8)
        cum = _dotf(hi, u) + _dotf(mid, u) + _dotf(lo, u)
        carry = fc_all[sq]
        fparts = []
        for p in range(npg):
            cp = cum[p * 8:(p + 1) * 8]
            fparts.append(cp + carry)
            carry = carry + cp[:, PAGE - 1:PAGE]
        fc_all[sq] = carry
        f8 = fparts[0] if npg == 1 else jnp.concatenate(fparts, axis=1)
        s_f = nt(qf_all[sq][:, 0:FD], f_b[:, 0:FD]) - LOG2E * jnp.concatenate([f8, f8], axis=0)
        if masked:
            key = lax.broadcasted_iota(jnp.int32, (32, w), 1) - (w - PAGE)
            tq = (lax.broadcasted_iota(jnp.int32, (32, w), 0) >> 2) & 3
            vis = key <= tq
            s_d = jnp.where(vis, s_d, NEG)
            s_m = jnp.where(vis[0:16], s_m, NEG)
            s_f = jnp.where(vis[0:16], s_f, NEG)
        a_d, p_d = soft(s_d, md_all.at[sq], ld_all.at[sq])
        accd_all[sq] = a_d * accd_all[sq] + _dotf(p_d, d_b[:, 2 * DQK:DIFF_ROW])
        a_m, p_m = soft(s_m, mm_all.at[sq], lm_all.at[sq])
        accm_all[sq] = a_m * accm_all[sq] + nt(p_m, mt_b[0:KV_LORA])
        a_f, p_f = soft(s_f, mf_all.at[sq], lff_all.at[sq])
        accf_all[sq] = a_f * accf_all[sq] + _dotf(p_f, f_b)

    def update_pair(blocks, npg, masked):
        for sq in range(NSEQ):
            update(sq, *blocks[sq], npg, masked)

    def stage_pages():
        for sq in range(NSEQ):
            for p in range(pp):
                q = sq * pp + p
                d_all[sq, p * PAGE:(p + 1) * PAGE, :] = dbuf[slot, q].astype(BF16)
                mt_all[sq, :, p * PAGE:(p + 1) * PAGE] = mbuf[slot, q].astype(BF16)
                f_all[sq, p * PAGE:(p + 1) * PAGE, :] = fbuf[slot, q].astype(BF16)
                lfp = lbuf[slot, q]
                lf_all[sq, p * 8:p * 8 + HEADS, :] = lfp
                lf_all[sq, p * 8 + HEADS:(p + 1) * 8, :] = lfp

    @pl.when(c < nc - 1)
    def _():
        stage_pages()
        update_pair([(d_all[sq], mt_all[sq], f_all[sq], lf_all[sq]) for sq in range(NSEQ)], pp, False)

    @pl.when(c == nc - 1)
    def _():
        stage_pages()
        lam = scal_ref[0]
        oml = scal_ref[1]
        update_pair([(jnp.concatenate([d_all[sq], nd_all[sq].astype(BF16)], axis=0),
                      jnp.concatenate([mt_all[sq], nm_all[sq].astype(BF16)], axis=1),
                      jnp.concatenate([f_all[sq], nf_all[sq].astype(BF16)], axis=0),
                      jnp.concatenate([lf_all[sq], nl_all[sq]], axis=0)) for sq in range(NSEQ)], pp + 1, True)
        for sq in range(NSEQ):
            od = accd_all[sq] * (1.0 / ld_all[sq])
            do = od[0:16] - lam * od[16:32]
            do = (do * lax.rsqrt(jnp.mean(do * do, axis=1, keepdims=True) + EPS)) * grow_ref[...] * oml
            od_all[sq] = do
            ctx = (accm_all[sq] * (1.0 / lm_all[sq])).astype(BF16)
            full = _dotf(ctx, wuv_ref[...])
            rh = lax.broadcasted_iota(jnp.int32, (16, 64), 0) & 3
            mo = full[:, 192:256]
            for h in range(HEADS - 1):
                mo = jnp.where(rh == h, full[:, h * 64:(h + 1) * 64], mo)
            om_all[sq] = mo
            of_all[sq] = (accf_all[sq] * (1.0 / lff_all[sq]))[:, FD:2 * FD]


def _decode_attention(layer, caches, page_table, q_tok, new_rows, lp, u_tri, *, pp):
    cache_diff, cache_mla_t, cache_fox, cache_logf_t = caches
    qd, qt, qr, qf = q_tok
    b, n_pages = page_table.shape
    assert n_pages % pp == 0
    nc = n_pages // pp
    ts = b * 4
    w = pp * PAGE
    qd5 = qd[:ts].reshape(b, 4, 2, HEADS, DQK)
    q1 = jnp.pad(qd5[:, :, 0], ((0, 0), (0, 0), (0, 0), (0, DQK)))
    q2 = jnp.pad(qd5[:, :, 1], ((0, 0), (0, 0), (0, 0), (DQK, 0)))
    qdr = jnp.stack([q1, q2], axis=1).reshape(b, 32, 2 * DQK)
    qmr = jnp.concatenate([
        jnp.pad(qt[:ts].reshape(b, 16, KV_LORA), ((0, 0), (0, 0), (0, ROPE))),
        jnp.pad(qr[:ts].reshape(b, 16, ROPE), ((0, 0), (0, 0), (KV_LORA, 0)))], axis=1)
    qfr = jnp.pad(qf[:ts].reshape(b, 16, FD), ((0, 0), (0, 0), (0, FD)))
    drs, mrs, frs, lfs = new_rows
    nd = jnp.pad(drs[:ts].reshape(b, 4, DIFF_ROW), ((0, 0), (0, PAGE - 4), (0, 0)))
    nm = jnp.pad(jnp.transpose(mrs[:ts].reshape(b, 4, MLA_ROW), (0, 2, 1)), ((0, 0), (0, 0), (0, PAGE - 4)))
    nf = jnp.pad(frs[:ts].reshape(b, 4, FOX_ROW), ((0, 0), (0, PAGE - 4), (0, 0)))
    nl = jnp.transpose(lfs[:ts, :HEADS].reshape(b, 4, HEADS), (0, 2, 1))
    nl = jnp.pad(jnp.concatenate([nl, nl], axis=1), ((0, 0), (0, 0), (0, PAGE - 4)))
    scal = jnp.stack([lp["lam"], jnp.asarray(1.0 - lp["lam_init"], F32)]).astype(F32)
    grow = lp["out_gain"].reshape(1, DV).astype(F32)

    in_specs = [pl.BlockSpec(memory_space=pl.ANY)] * 4
    operands = [cache_diff, cache_mla_t, cache_fox, cache_logf_t]
    assert b % NSEQ == 0
    npair = b // NSEQ
    seq = lambda r, wd: pl.BlockSpec((NSEQ, r, wd), lambda bi, ci, pt: (bi, 0, 0))
    cst = lambda shape: pl.BlockSpec(shape, lambda bi, ci, pt, _n=len(shape): (0,) * _n)
    in_specs += [seq(32, 2 * DQK), seq(32, MLA_ROW), seq(16, FOX_ROW),
                 seq(PAGE, DIFF_ROW), seq(MLA_ROW, PAGE), seq(PAGE, FOX_ROW), seq(8, PAGE),
                 cst((HEADS * NOPE, MLA_ROW)), cst((PAGE, PAGE)), cst((KV_LORA, HEADS * 64)), cst((1, DV)),
                 pl.BlockSpec(memory_space=pltpu.SMEM)]
    operands += [qdr, qmr, qfr, nd, nm, nf, nl, lp["wukt_pad"], u_tri, lp["wuv_all"], grow, scal]
    col = lambda r: pltpu.VMEM((NSEQ, r, 1), F32)
    npb = NSEQ * pp
    grid_spec = pltpu.PrefetchScalarGridSpec(
        num_scalar_prefetch=1,
        grid=(npair, nc),
        in_specs=in_specs,
        out_specs=[seq(16, DV), seq(16, 64), seq(16, FD)],
        scratch_shapes=[pltpu.VMEM((2, npb, PAGE, DIFF_ROW), F32), pltpu.VMEM((2, npb, MLA_ROW, PAGE), F32),
                        pltpu.VMEM((2, npb, PAGE, FOX_ROW), F32), pltpu.VMEM((2, npb, HEADS, PAGE), F32),
                        pltpu.SemaphoreType.DMA((2, 4)),
                        pltpu.VMEM((NSEQ, HEADS * NOPE + 32, MLA_ROW), BF16),
                        pltpu.VMEM((NSEQ, w, DIFF_ROW), BF16), pltpu.VMEM((NSEQ, MLA_ROW, w), BF16),
                        pltpu.VMEM((NSEQ, w, FOX_ROW), BF16), pltpu.VMEM((NSEQ, pp * 8, PAGE), F32),
                        col(32), col(32), col(16), col(16), col(16), col(16),
                        pltpu.VMEM((NSEQ, 32, DV), F32), pltpu.VMEM((NSEQ, 16, KV_LORA), F32),
                        pltpu.VMEM((NSEQ, 16, FOX_ROW), F32), col(8)],
    )
    od, om, of = pl.pallas_call(
        functools.partial(_decode_kernel, layer, pp, nc, n_pages, npair * nc),
        grid_spec=grid_spec,
        out_shape=[jax.ShapeDtypeStruct((b, 16, DV), F32), jax.ShapeDtypeStruct((b, 16, 64), F32),
                   jax.ShapeDtypeStruct((b, 16, FD), F32)],
        compiler_params=pltpu.CompilerParams(dimension_semantics=("arbitrary", "arbitrary"),
                                             vmem_limit_bytes=VMEM_LIMIT),
        name="decode_attention",
    )(page_table.reshape(-1), *operands)
    return jnp.concatenate([od.reshape(b, 4, HEADS * DV), om.reshape(b, 4, HEADS * 64),
                            of.reshape(b, 4, HEADS * FD)], axis=-1).reshape(ts, MIX)


def _round_up(n, m):
    return (n + m - 1) // m * m


def _largest_tile(n, options):
    for o in options:
        if n % o == 0:
            return o
    raise ValueError(f"no tile for {n}")


def kernel(x_prompt, x_sample, cache_diff, cache_mla, cache_fox_kv, cache_fox_logf, page_table, meta_tokens, norm_gain, w_in, w_out, diff_q_gain, diff_k_gain, diff_lambda, diff_out_gain, mla_q_a_gain, mla_w_uq, mla_kv_gain, mla_w_uk, mla_w_uv, mla_q_nope_gain, mla_q_rope_gain, mla_k_nope_gain, mla_k_rope_gain, fox_q_gain, fox_k_gain, fox_f_bias):
    assert x_prompt.shape[0] == 1
    depth = w_in.shape[0]
    seq = x_prompt.shape[1]
    t_real = N_META + seq
    tb = 512 if t_real >= 4096 else 128
    t_pad = _round_up(t_real, tb)
    tm_p = min(256, tb)
    b, t_new = x_sample.shape[0], x_sample.shape[1]
    assert t_new == 4
    n_pages = page_table.shape[1]
    past_len = n_pages * PAGE
    ts = b * t_new
    ts_pad = _round_up(ts, 16)
    tm_s = _largest_tile(ts_pad, (256, 128, 64, 32, 16))
    pp = _largest_tile(n_pages, (32, 16, 8, 4, 2, 1))

    xp = jnp.concatenate([meta_tokens.astype(F32), x_prompt[0], jnp.zeros((t_pad - t_real, D_MODEL), F32)], axis=0)
    pos_p = jnp.arange(t_pad, dtype=F32).reshape(t_pad, 1)
    xs = jnp.pad(x_sample.reshape(ts, D_MODEL), ((0, ts_pad - ts), (0, 0)))
    pos_s = (past_len + (jnp.arange(ts_pad) % t_new)).astype(F32).reshape(ts_pad, 1)

    inv64, inv32 = _inv_freq_rows()
    g64 = jnp.asarray(_block_diag_mean(256, 64), BF16)
    g32 = jnp.asarray(_block_diag_mean(128, 32), BF16)
    tri_p = jnp.asarray(np.tril(np.ones((tm_p, tm_p), np.float32)), BF16)
    tri_s = jnp.asarray(np.tril(np.ones((tm_s, tm_s), np.float32)), BF16)
    u_pg = jnp.asarray(np.triu(np.ones((PAGE, PAGE), np.float32)), BF16)
    caches = (cache_diff, jnp.swapaxes(cache_mla, 2, 3), cache_fox_kv, jnp.swapaxes(cache_fox_logf, 2, 3))

    rows_p, rows_s = [], []
    for l in range(depth):
        lp = _layer_params(l, norm_gain, w_in, w_out, diff_q_gain, diff_k_gain, diff_lambda, diff_out_gain,
                           mla_q_a_gain, mla_w_uq, mla_kv_gain, mla_w_uk, mla_w_uv, mla_q_nope_gain,
                           mla_q_rope_gain, mla_k_nope_gain, mla_k_rope_gain, fox_q_gain, fox_k_gain, fox_f_bias)
        po = _proj(xp, pos_p, lp, (inv64, inv32, g64, g32, tri_p), prompt=True, tm=tm_p)
        ops, (drow, mrow, frow, logf, sg) = po[:9], po[9:]
        mixed = _prompt_attention(ops, lp, tb=tb)
        xp = _out_proj(xp, mixed, sg, lp["wout"], tm=tm_p)
        rows_p.append((drow[:t_real], mrow[:t_real], frow[:t_real], logf[:t_real, :HEADS]))

        so = _proj(xs, pos_s, lp, (inv64, inv32, g64, g32, tri_s), prompt=False, tm=tm_s)
        q_tok, (drs, mrs, frs, lfs, sgs) = so[:4], so[4:]
        mixed_s = _decode_attention(l, caches, page_table, q_tok, (drs, mrs, frs, lfs), lp, u_pg, pp=pp)
        mixed_s = jnp.pad(mixed_s, ((0, ts_pad - ts), (0, 0)))
        xs = _out_proj(xs, mixed_s, sgs, lp["wout"], tm=tm_s)
        rows_s.append((drs[:ts], mrs[:ts], frs[:ts], lfs[:ts, :HEADS]))

    y_prompt = xp[N_META:t_real][None]
    y_sample = xs[:ts].reshape(b, t_new, D_MODEL)
    outs_p = [jnp.stack([r[k] for r in rows_p])[:, None] for k in range(4)]
    outs_s = [jnp.stack([r[k].reshape(b, t_new, -1) for r in rows_s]) for k in range(4)]
    return (y_prompt, y_sample, *outs_p, *outs_s)
```
